```python
import math
import jax
import jax.numpy as jnp
from jax import lax
import numpy as np

D_MODEL = 4096
BATCH = 4
SEQ = 2048
DEPTH = 2

HEAD_DIM = 128
N_Q_HEADS = D_MODEL // (2 * HEAD_DIM)
N_KV_HEADS = N_Q_HEADS // 4
HEADS_PER_GROUP = N_Q_HEADS // N_KV_HEADS
Q_WIDTH = N_Q_HEADS * HEAD_DIM
KV_WIDTH = N_KV_HEADS * HEAD_DIM
N_NSA_BRANCHES = 3
ATTN_SCALE = HEAD_DIM ** -0.5
ROPE_THETA = 500000.0
ROPE_DIM = HEAD_DIM // 4
CMP_BLOCK = 32
CMP_STRIDE = 16
CMP_HIDDEN = 256
SEL_BLOCK = 64
N_SELECT = 16
N_LOCAL_SEL = 2
WINDOW = 512
WIN_Q_BLOCK = 128
SEL_Q_BLOCK = 16
RNN_WIDTH = D_MODEL // 2
RNN_HEADS = 16
RNN_BLOCK = RNN_WIDTH // RNN_HEADS
CONV_WIDTH = 4
LRU_C = 8.0
N_MIX_BRANCHES = 2
IN_WIDTH = Q_WIDTH + 2 * N_NSA_BRANCHES * KV_WIDTH + N_NSA_BRANCHES * N_Q_HEADS + 2 * RNN_WIDTH + N_MIX_BRANCHES * D_MODEL
N_EXPERTS = 32
TOP_K = 4
EXPERT_FF = 512
SWIGLU_LIMIT = 7.0
SWIGLU_ALPHA = 1.702
MOE_TOKEN_BLOCK = 1024
N_ADA = 6
NORM_EPS = 1e-6
NEG_INF = -1e30
FORCED_SCORE = 1e6

kernel_name = 'hybrid_nsa_rglru_moe_adaln'


def rms_norm(x, g):
    xf = x.astype(jnp.float32)
    y = xf * lax.rsqrt(jnp.mean(xf * xf, axis=-1, keepdims=True) + NORM_EPS)
    return (y * g.astype(jnp.float32)).astype(x.dtype)


def rope_partial(x, pos):
    half = ROPE_DIM // 2
    inv_freq = jnp.power(ROPE_THETA, -jnp.arange(half, dtype=jnp.float32) * 2.0 / ROPE_DIM)
    ang = pos[:, None] * inv_freq[None, :]
    cos, sin = jnp.cos(ang), jnp.sin(ang)
    xf = x.astype(jnp.float32)
    x1, x2 = xf[..., :half], xf[..., half:ROPE_DIM]
    out = jnp.concatenate([x1 * cos - x2 * sin, x1 * sin + x2 * cos, xf[..., ROPE_DIM:]], axis=-1)
    return out.astype(x.dtype)


def masked_softmax(s, mask):
    p = jax.nn.softmax(jnp.where(mask, s, NEG_INF), axis=-1)
    return jnp.where(mask, p, 0.0)


def split_columns(z):
    sizes = [Q_WIDTH] + [KV_WIDTH] * (2 * N_NSA_BRANCHES) + [N_NSA_BRANCHES * N_Q_HEADS, RNN_WIDTH, RNN_WIDTH] + [D_MODEL] * N_MIX_BRANCHES
    cuts = np.cumsum(sizes)[:-1].tolist()
    return jnp.split(z, cuts, axis=-1)


def nsa_attention(q, k_cmp_in, v_cmp_in, k_sel_in, v_sel_in, k_win_in, v_win_in, gate_logits, w_cmp1, w_cmp2, cmp_pos):
    B, T, _ = q.shape
    G, HPG, HD = N_KV_HEADS, HEADS_PER_GROUP, HEAD_DIM
    f32 = jnp.float32
    pos = jnp.arange(T, dtype=f32)
    t_idx = jnp.arange(T)

    def to_heads(z, n):
        return z.reshape(B, T, n, HD).transpose(0, 2, 1, 3)

    qh = rope_partial(to_heads(q, N_Q_HEADS), pos).reshape(B, G, HPG, T, HD)

    n_cmp = (T - CMP_BLOCK) // CMP_STRIDE + 1
    cmp_start = np.arange(n_cmp) * CMP_STRIDE
    cmp_end = cmp_start + CMP_BLOCK - 1
    tok_idx = cmp_start[:, None] + np.arange(CMP_BLOCK)[None, :]

    def compress(z, j):
        blocks = z[:, :, tok_idx] + cmp_pos[j]
        flat = blocks.reshape(B, G, n_cmp, CMP_BLOCK * HD)
        return jax.nn.gelu(flat @ w_cmp1[j]) @ w_cmp2[j]

    k_c = rope_partial(compress(to_heads(k_cmp_in, G), 0), jnp.asarray(cmp_end, f32))
    v_c = compress(to_heads(v_cmp_in, G), 1)
    s_c = jnp.einsum('bghtd,bgnd->bghtn', qh, k_c, preferred_element_type=f32) * ATTN_SCALE
    p_c = masked_softmax(s_c, jnp.asarray(cmp_end)[None, :] <= t_idx[:, None])
    o_cmp = jnp.einsum('bghtn,bgnd->bghtd', p_c.astype(v_c.dtype), v_c)

    n_slc = T // SEL_BLOCK
    n_sel = min(N_SELECT, n_slc)
    slc_start = np.arange(n_slc) * SEL_BLOCK
    overlap = np.clip(np.minimum(cmp_start[:, None] + CMP_BLOCK, slc_start[None, :] + SEL_BLOCK)
                      - np.maximum(cmp_start[:, None], slc_start[None, :]), 0, None) / CMP_BLOCK
    imp = jnp.einsum('bghtn,nj->bgtj', p_c, jnp.asarray(overlap, f32))
    blk = jnp.arange(n_slc)[None, :]
    cur = (t_idx // SEL_BLOCK)[:, None]
    visible = blk <= cur
    forced = (blk == 0) | ((blk <= cur) & (blk > cur - N_LOCAL_SEL))
    score = jnp.where(visible, jnp.where(forced, FORCED_SCORE, imp), NEG_INF)
    top_val, top_idx = lax.top_k(score, n_sel)
    top_ok = top_val > 0.5 * NEG_INF

    k_s = rope_partial(to_heads(k_sel_in, G), pos).reshape(B, G, n_slc, SEL_BLOCK, HD)
    v_s = to_heads(v_sel_in, G).reshape(B, G, n_slc, SEL_BLOCK, HD)
    gather = jax.vmap(jax.vmap(lambda blocks, ix: blocks[ix]))
    n_sqb = T // SEL_Q_BLOCK
    n_keys = n_sel * SEL_BLOCK
    key_off = jnp.arange(SEL_BLOCK)

    def sel_block(args):
        i, qb, ib, okb = args
        tq = i * SEL_Q_BLOCK + jnp.arange(SEL_Q_BLOCK)
        kk = gather(k_s, ib).reshape(B, G, SEL_Q_BLOCK, n_keys, HD)
        vv = gather(v_s, ib).reshape(B, G, SEL_Q_BLOCK, n_keys, HD)
        kpos = (ib[..., None] * SEL_BLOCK + key_off).reshape(B, G, SEL_Q_BLOCK, n_keys)
        ok = jnp.repeat(okb, SEL_BLOCK, axis=-1) & (kpos <= tq[:, None])
        s = jnp.einsum('bghqd,bgqkd->bghqk', qb, kk, preferred_element_type=f32) * ATTN_SCALE
        p = masked_softmax(s, ok[:, :, None])
        return jnp.einsum('bghqk,bgqkd->bghqd', p.astype(vv.dtype), vv)

    o_sel = lax.map(sel_block, (jnp.arange(n_sqb),
                                qh.reshape(B, G, HPG, n_sqb, SEL_Q_BLOCK, HD).transpose(3, 0, 1, 2, 4, 5),
                                top_idx.reshape(B, G, n_sqb, SEL_Q_BLOCK, n_sel).transpose(2, 0, 1, 3, 4),
                                top_ok.reshape(B, G, n_sqb, SEL_Q_BLOCK, n_sel).transpose(2, 0, 1, 3, 4)))
    o_sel = o_sel.transpose(1, 2, 3, 0, 4, 5).reshape(B, G, HPG, T, HD)

    pad = ((0, 0), (0, 0), (WINDOW, 0), (0, 0))
    k_w = jnp.pad(rope_partial(to_heads(k_win_in, G), pos), pad)
    v_w = jnp.pad(to_heads(v_win_in, G), pad)
    n_wqb = T // WIN_Q_BLOCK
    span = WIN_Q_BLOCK + WINDOW

    def win_block(args):
        i, qb = args
        start = i * WIN_Q_BLOCK
        kk = lax.dynamic_slice_in_dim(k_w, start, span, axis=2)
        vv = lax.dynamic_slice_in_dim(v_w, start, span, axis=2)
        tq = (start + jnp.arange(WIN_Q_BLOCK))[:, None]
        kp = (start - WINDOW + jnp.arange(span))[None, :]
        ok = (kp <= tq) & (kp > tq - WINDOW) & (kp >= 0)
        s = jnp.einsum('bghqd,bgkd->bghqk', qb, kk, preferred_element_type=f32) * ATTN_SCALE
        p = masked_softmax(s, ok)
        return jnp.einsum('bghqk,bgkd->bghqd', p.astype(vv.dtype), vv)

    o_win = lax.map(win_block, (jnp.arange(n_wqb),
                                qh.reshape(B, G, HPG, n_wqb, WIN_Q_BLOCK, HD).transpose(3, 0, 1, 2, 4, 5)))
    o_win = o_win.transpose(1, 2, 3, 0, 4, 5).reshape(B, G, HPG, T, HD)

    g = jax.nn.sigmoid(gate_logits.astype(f32)).reshape(B, T, N_Q_HEADS, N_NSA_BRANCHES)
    g = g.transpose(0, 2, 1, 3).reshape(B, G, HPG, T, N_NSA_BRANCHES)
    o = g[..., 0:1] * o_cmp + g[..., 1:2] * o_sel + g[..., 2:3] * o_win
    return o.reshape(B, N_Q_HEADS, T, HD).transpose(0, 2, 1, 3).reshape(B, T, Q_WIDTH).astype(q.dtype)


def rglru_branch(x_rnn, g_rnn, conv_w, conv_b, w_gate, b_gate, lam):
    B, T, _ = x_rnn.shape
    f32 = jnp.float32
    xc = lax.conv_general_dilated(x_rnn, conv_w.astype(x_rnn.dtype)[:, None, :], window_strides=(1,),
                                  padding=[(CONV_WIDTH - 1, 0)], dimension_numbers=('NWC', 'WIO', 'NWC'),
                                  feature_group_count=RNN_WIDTH) + conv_b
    xb = xc.reshape(B, T, RNN_HEADS, RNN_BLOCK)
    gates = jnp.einsum('bthi,ghij->gbthj', xb, w_gate).reshape(2, B, T, RNN_WIDTH) + b_gate[:, None, None, :]
    r = jax.nn.sigmoid(gates[0].astype(f32))
    i = jax.nn.sigmoid(gates[1].astype(f32))
    log_a = -LRU_C * r * jax.nn.softplus(-lam.astype(f32))
    a = jnp.exp(log_a)
    b = jnp.sqrt(-jnp.expm1(2.0 * log_a)) * (i * xc.astype(f32))

    def combine(left, right):
        a1, b1 = left
        a2, b2 = right
        return a1 * a2, a2 * b1 + b2

    _, h = lax.associative_scan(combine, (a, b), axis=1)
    return (h * jax.nn.gelu(g_rnn.astype(f32))).astype(x_rnn.dtype)


def moe_ffn(h, w_router, b_router, w1, b1, w2, b2):
    B, T, D = h.shape
    n_tok = B * T
    hf = h.reshape(n_tok, D)
    logits = (hf @ w_router).astype(jnp.float32) + b_router
    top_val, top_idx = lax.top_k(logits, TOP_K)
    wts = jax.nn.softmax(top_val, axis=-1)
    gates = jnp.sum(jax.nn.one_hot(top_idx, N_EXPERTS, dtype=jnp.float32) * wts[..., None], axis=1)
    ch = math.gcd(n_tok, MOE_TOKEN_BLOCK)

    def expert_block(args):
        hc, gc = args
        z = jnp.einsum('td,edf->tef', hc, w1) + b1
        glu = jnp.minimum(z[..., ::2], SWIGLU_LIMIT)
        lin = jnp.clip(z[..., 1::2], -SWIGLU_LIMIT, SWIGLU_LIMIT)
        act = glu * jax.nn.sigmoid(SWIGLU_ALPHA * glu) * (lin + 1.0)
        return jnp.einsum('tef,efd->td', act * gc[..., None].astype(act.dtype), w2) + gc.astype(b2.dtype) @ b2

    y = lax.map(expert_block, (hf.reshape(n_tok // ch, ch, D), gates.reshape(n_tok // ch, ch, N_EXPERTS)))
    return y.reshape(B, T, D).astype(h.dtype)


def setup_inputs(seed: int = 0) -> dict:
    key = jax.random.key(seed)
    k = jax.random.split(key, 26)
    f32 = jnp.float32
    D = D_MODEL

    def nrm(kk, shape, s):
        return jax.random.normal(kk, shape, f32) * s

    u = jax.random.uniform(k[15], (DEPTH, RNN_WIDTH), f32, minval=0.9, maxval=0.999)
    a0 = u ** (1.0 / LRU_C)
    lru_lambda = jnp.log(a0) - jnp.log1p(-a0)
    return {
        'x': nrm(k[0], (BATCH, SEQ, D), 1.0),
        'c': nrm(k[1], (BATCH, D), 1.0),
        'w_ada': nrm(k[2], (D, N_ADA * D), 0.5 * D ** -0.5),
        'b_ada': nrm(k[3], (N_ADA * D,), 0.01),
        'ada_table': nrm(k[4], (DEPTH, N_ADA, D), 0.1),
        'g_norm1': 1.0 + nrm(k[5], (DEPTH, D), 0.01),
        'w_in': nrm(k[6], (DEPTH, D, IN_WIDTH), D ** -0.5),
        'w_cmp1': nrm(k[7], (DEPTH, 2, CMP_BLOCK * HEAD_DIM, CMP_HIDDEN), (CMP_BLOCK * HEAD_DIM) ** -0.5),
        'w_cmp2': nrm(k[8], (DEPTH, 2, CMP_HIDDEN, HEAD_DIM), CMP_HIDDEN ** -0.5),
        'cmp_pos': nrm(k[9], (DEPTH, 2, CMP_BLOCK, HEAD_DIM), 0.1),
        'conv_w': nrm(k[10], (DEPTH, CONV_WIDTH, RNN_WIDTH), CONV_WIDTH ** -0.5),
        'conv_b': nrm(k[11], (DEPTH, RNN_WIDTH), 0.01),
        'w_lru_gate': nrm(k[12], (DEPTH, 2, RNN_HEADS, RNN_BLOCK, RNN_BLOCK), RNN_BLOCK ** -0.5),
        'b_lru_gate': nrm(k[13], (DEPTH, 2, RNN_WIDTH), 0.01),
        'lru_lambda': lru_lambda,
        'w_branch': nrm(k[16], (DEPTH, N_MIX_BRANCHES, Q_WIDTH, D), Q_WIDTH ** -0.5),
        'w_out': nrm(k[17], (DEPTH, D, D), D ** -0.5),
        'g_norm2': 1.0 + nrm(k[18], (DEPTH, D), 0.01),
        'w_router': nrm(k[19], (DEPTH, D, N_EXPERTS), D ** -0.5),
        'b_router': nrm(k[20], (DEPTH, N_EXPERTS), 0.01),
        'w_exp1': nrm(k[21], (DEPTH, N_EXPERTS, D, 2 * EXPERT_FF), D ** -0.5),
        'b_exp1': nrm(k[22], (DEPTH, N_EXPERTS, 2 * EXPERT_FF), 0.01),
        'w_exp2': nrm(k[23], (DEPTH, N_EXPERTS, EXPERT_FF, D), EXPERT_FF ** -0.5),
        'b_exp2': nrm(k[24], (DEPTH, N_EXPERTS, D), 0.01),
        'g_final': 1.0 + nrm(k[25], (D,), 0.01),
    }


def reference(x, c, w_ada, b_ada, ada_table, g_norm1, w_in, w_cmp1, w_cmp2, cmp_pos, conv_w, conv_b,
              w_lru_gate, b_lru_gate, lru_lambda, w_branch, w_out, g_norm2, w_router, b_router,
              w_exp1, b_exp1, w_exp2, b_exp2, g_final):
    B = x.shape[0]
    emb = (jax.nn.silu(c) @ w_ada + b_ada).reshape(B, N_ADA, D_MODEL)
    for l in range(DEPTH):
        mod = emb + ada_table[l][None]
        shift1, scale1, gate1, shift2, scale2, gate2 = (mod[:, j, None, :] for j in range(N_ADA))
        h = rms_norm(x, g_norm1[l]) * (1.0 + scale1) + shift1
        (q, kc, vc, ks_, vs_, kw, vw, g_nsa, x_rnn, g_rnn, m_attn, m_rnn) = split_columns(h @ w_in[l])
        o_attn = nsa_attention(q, kc, vc, ks_, vs_, kw, vw, g_nsa, w_cmp1[l], w_cmp2[l], cmp_pos[l])
        o_rnn = rglru_branch(x_rnn, g_rnn, conv_w[l], conv_b[l], w_lru_gate[l], b_lru_gate[l], lru_lambda[l])
        mixed = jax.nn.sigmoid(m_attn) * (o_attn @ w_branch[l, 0]) + jax.nn.sigmoid(m_rnn) * (o_rnn @ w_branch[l, 1])
        x = x + gate1 * (mixed @ w_out[l])
        h2 = rms_norm(x, g_norm2[l]) * (1.0 + scale2) + shift2
        x = x + gate2 * moe_ffn(h2, w_router[l], b_router[l], w_exp1[l], b_exp1[l], w_exp2[l], b_exp2[l])
    return rms_norm(x, g_final)
```

```python
import functools
import math

import numpy as np
import jax
import jax.numpy as jnp
from jax import lax
from jax.experimental import pallas as pl
from jax.experimental.pallas import tpu as pltpu

F32 = jnp.float32
BF16 = jnp.bfloat16

D_MODEL = 4096
HEAD_DIM = 128
N_Q_HEADS = 16
N_KV_HEADS = 4
HEADS_PER_GROUP = 4
Q_WIDTH = N_Q_HEADS * HEAD_DIM
KV_WIDTH = N_KV_HEADS * HEAD_DIM
ATTN_SCALE = HEAD_DIM ** -0.5
ROPE_THETA = 500000.0
ROPE_DIM = 32
CMP_BLOCK = 32
CMP_STRIDE = 16
CMP_HIDDEN = 256
SEL_BLOCK = 64
N_SELECT = 16
N_LOCAL_SEL = 2
WINDOW = 512
RNN_WIDTH = 2048
RNN_HEADS = 16
CONV_WIDTH = 4
LRU_C = 8.0
N_EXPERTS = 32
TOP_K = 4
EXPERT_FF = 512
SWIGLU_LIMIT = 7.0
SWIGLU_ALPHA = 1.702
N_ADA = 6
NORM_EPS = 1e-6
NEG_INF = -1e30
FORCED_SCORE = 1e6

SEG_A = Q_WIDTH + 6 * KV_WIDTH
GATE_W = 3 * N_Q_HEADS
SEG_R0 = SEG_A + GATE_W
SEG_R = 2 * RNN_WIDTH + 2 * D_MODEL

LANES = 128
VMEM_LIMIT_MB = 56
MM_TM = 1024
MM_TN = 512
ATT_TQ = 256
ATT_TK = 256
RNN_TT = 256
RNN_CB = 256
NORM_TM = 256
MOE_TM = 256
COMB_TT = 128


def _cparams(sem):
    return pltpu.CompilerParams(dimension_semantics=sem, vmem_limit_bytes=VMEM_LIMIT_MB << 20)


def _sigmoid(x):
    return 1.0 / (1.0 + jnp.exp(-x))


def _gelu_tanh(x):
    return 0.5 * x * (1.0 + jnp.tanh(0.7978845608028654 * (x + 0.044715 * (x * x * x))))


def _bdot(a, b):
    return jnp.dot(a, b, preferred_element_type=F32)


def _split_bf16(x):
    hi = x.astype(BF16)
    lo = (x - hi.astype(F32)).astype(BF16)
    return hi, lo


def _ada_kernel(c_ref, w_ref, b_ref, t_ref, o_ref):
    c = c_ref[...]
    s = (c * _sigmoid(c)).astype(BF16)
    emb = _bdot(s, w_ref[...].astype(BF16)) + b_ref[...]
    for l in range(o_ref.shape[0]):
        o_ref[l] = emb + t_ref[l]


def _ada_mod(c, w_ada, b_ada, ada_table):
    depth = ada_table.shape[0]
    b = c.shape[0]
    n = w_ada.shape[1]
    c8 = jnp.zeros((8, D_MODEL), F32).at[:b].set(c)
    tn = 512
    return pl.pallas_call(
        _ada_kernel,
        grid=(n // tn,),
        in_specs=[
            pl.BlockSpec((8, D_MODEL), lambda j: (0, 0)),
            pl.BlockSpec((D_MODEL, tn), lambda j: (0, j)),
            pl.BlockSpec((1, tn), lambda j: (0, j)),
            pl.BlockSpec((depth, 1, tn), lambda j: (0, 0, j)),
        ],
        out_specs=pl.BlockSpec((depth, 8, tn), lambda j: (0, 0, j)),
        out_shape=jax.ShapeDtypeStruct((depth, 8, n), F32),
        compiler_params=_cparams(("arbitrary",)),
        name="ada_mod",
    )(c8, w_ada, b_ada.reshape(1, n), ada_table.reshape(depth, 1, n))


def _norm_mod(x, g, sc, sh):
    r = lax.rsqrt(jnp.mean(x * x, axis=-1, keepdims=True) + NORM_EPS)
    return ((x * r) * g) * (1.0 + sc) + sh


def _norm_kernel(x_ref, g_ref, sc_ref, sh_ref, o_ref):
    o_ref[...] = _norm_mod(x_ref[...], g_ref[...], sc_ref[...], sh_ref[...]).astype(o_ref.dtype)


def _norm_call(x, g, sc, sh, seq):
    n, d = x.shape
    tm = NORM_TM
    per_b = seq // tm
    return pl.pallas_call(
        _norm_kernel,
        grid=(n // tm,),
        in_specs=[
            pl.BlockSpec((tm, d), lambda i: (i, 0)),
            pl.BlockSpec((1, d), lambda i: (0, 0)),
            pl.BlockSpec((None, 1, d), lambda i: (i // per_b, 0, 0)),
            pl.BlockSpec((None, 1, d), lambda i: (i // per_b, 0, 0)),
        ],
        out_specs=pl.BlockSpec((tm, d), lambda i: (i, 0)),
        out_shape=jax.ShapeDtypeStruct((n, d), BF16),
        compiler_params=_cparams(("arbitrary",)),
        name="norm_mod",
    )(x, g.reshape(1, d), sc, sh)


def _rope_tables(seq):
    half = ROPE_DIM // 2
    inv_freq = jnp.power(ROPE_THETA, -jnp.arange(half, dtype=F32) * 2.0 / ROPE_DIM)
    ang = jnp.arange(seq, dtype=F32)[:, None] * inv_freq[None, :]
    cos, sin = jnp.cos(ang), jnp.sin(ang)
    ones = jnp.ones((seq, HEAD_DIM - ROPE_DIM), F32)
    c_k = jnp.concatenate([cos, cos, ones], axis=1)
    s_k = jnp.concatenate([-sin, sin, 0.0 * ones], axis=1)
    c_id = jnp.ones((seq, HEAD_DIM), F32)
    s_id = jnp.zeros((seq, HEAD_DIM), F32)
    return (jnp.stack([c_k * ATTN_SCALE, c_k, c_id]), jnp.stack([s_k * ATTN_SCALE, s_k, s_id]))


def _rope_apply(x, cos, sin):
    heads = x.shape[1] // HEAD_DIM
    lane = lax.broadcasted_iota(jnp.int32, x.shape, 1) & (HEAD_DIM - 1)
    half = ROPE_DIM // 2
    fwd = pltpu.roll(x, half, 1)
    bwd = pltpu.roll(x, x.shape[1] - half, 1)
    sw = jnp.where(lane < half, bwd, fwd)
    if heads > 1:
        cos = jnp.concatenate([cos] * heads, axis=1)
        sin = jnp.concatenate([sin] * heads, axis=1)
    return x * cos + sw * sin


def _proj_rope_kernel(a_ref, w_ref, cos_ref, sin_ref, o_ref):
    acc = _bdot(a_ref[...], w_ref[...].astype(BF16))
    o_ref[...] = _rope_apply(acc, cos_ref[...], sin_ref[...]).astype(o_ref.dtype)


def _proj_kernel(a_ref, w_ref, o_ref):
    o_ref[...] = _bdot(a_ref[...], w_ref[...].astype(BF16)).astype(o_ref.dtype)


def _proj_a(h, w_in, layer, cos_t, sin_t, seq):
    n, d = h.shape
    tm, tn = MM_TM, MM_TN
    per_b = seq // tm

    def variant(j):
        return jnp.where(j < 4, 0, jnp.where((j == 6) | (j == 8), 1, 2))

    return pl.pallas_call(
        _proj_rope_kernel,
        grid=(n // tm, SEG_A // tn),
        in_specs=[
            pl.BlockSpec((tm, d), lambda i, j: (i, 0)),
            pl.BlockSpec((None, d, tn), lambda i, j: (layer, 0, j)),
            pl.BlockSpec((None, tm, HEAD_DIM), lambda i, j: (variant(j), i % per_b, 0)),
            pl.BlockSpec((None, tm, HEAD_DIM), lambda i, j: (variant(j), i % per_b, 0)),
        ],
        out_specs=pl.BlockSpec((tm, tn), lambda i, j: (i, j)),
        out_shape=jax.ShapeDtypeStruct((n, SEG_A), BF16),
        compiler_params=_cparams(("arbitrary", "arbitrary")),
        name="proj_qkv",
    )(h, w_in, cos_t, sin_t)


def _proj_gate(h, w_in, layer):
    n, d = h.shape
    tm = MM_TM
    return pl.pallas_call(
        _proj_kernel,
        grid=(n // tm,),
        in_specs=[
            pl.BlockSpec((tm, d), lambda i: (i, 0)),
            pl.BlockSpec((None, d, LANES), lambda i: (layer, 0, SEG_A // LANES)),
        ],
        out_specs=pl.BlockSpec((tm, LANES), lambda i: (i, 0)),
        out_shape=jax.ShapeDtypeStruct((n, LANES), F32),
        compiler_params=_cparams(("arbitrary",)),
        name="proj_gate",
    )(h, w_in)


def _proj_r(h, w_r):
    n, d = h.shape
    tm, tn = MM_TM, MM_TN
    return pl.pallas_call(
        _proj_kernel,
        grid=(n // tm, SEG_R // tn),
        in_specs=[
            pl.BlockSpec((tm, d), lambda i, j: (i, 0)),
            pl.BlockSpec((d, tn), lambda i, j: (0, j)),
        ],
        out_specs=pl.BlockSpec((tm, tn), lambda i, j: (i, j)),
        out_shape=jax.ShapeDtypeStruct((n, SEG_R), F32),
        compiler_params=_cparams(("arbitrary", "arbitrary")),
        name="proj_rnn_mix",
    )(h, w_r)


def _cmp_kernel(z_ref, w1_ref, w2_ref, pos_ref, cos_ref, sin_ref, o_ref):
    z = z_ref[...]
    w1 = w1_ref[...].astype(BF16)
    half = w1.shape[0] // 2
    a = _bdot(z, w1[:half])
    b = _bdot(z, w1[half:])
    pos = jnp.broadcast_to(pos_ref[...], (8, pos_ref.shape[1])).astype(BF16)
    bias = _bdot(pos, w1)[0:1]
    y = a + pltpu.roll(b, b.shape[0] - 1, 0) + bias
    hid = _gelu_tanh(y).astype(BF16)
    out = _bdot(hid, w2_ref[...].astype(BF16))
    groups = out.shape[0] // cos_ref.shape[0]
    cos = jnp.concatenate([cos_ref[...]] * groups, axis=0)
    sin = jnp.concatenate([sin_ref[...]] * groups, axis=0)
    o_ref[...] = _rope_apply(out, cos, sin).astype(o_ref.dtype)


def _compress(z2, w_cmp1, w_cmp2, cmp_pos, layer, cos_c, sin_c):
    _, b, rows, width = z2.shape
    n_grp = cos_c.shape[1]
    return pl.pallas_call(
        _cmp_kernel,
        grid=(2, b),
        in_specs=[
            pl.BlockSpec((None, None, rows, width), lambda j, bi: (j, bi, 0, 0)),
            pl.BlockSpec((None, None, 2 * width, CMP_HIDDEN), lambda j, bi: (layer, j, 0, 0)),
            pl.BlockSpec((None, None, CMP_HIDDEN, HEAD_DIM), lambda j, bi: (layer, j, 0, 0)),
            pl.BlockSpec((None, None, 1, 2 * width), lambda j, bi: (layer, j, 0, 0)),
            pl.BlockSpec((None, n_grp, HEAD_DIM), lambda j, bi: (j, 0, 0)),
            pl.BlockSpec((None, n_grp, HEAD_DIM), lambda j, bi: (j, 0, 0)),
        ],
        out_specs=pl.BlockSpec((None, None, rows, HEAD_DIM), lambda j, bi: (j, bi, 0, 0)),
        out_shape=jax.ShapeDtypeStruct((2, b, rows, HEAD_DIM), BF16),
        compiler_params=_cparams(("arbitrary", "arbitrary")),
        name="nsa_compress",
    )(z2, w_cmp1, w_cmp2, cmp_pos.reshape(cmp_pos.shape[0], 2, 1, CMP_BLOCK * HEAD_DIM), cos_c, sin_c)


def _attn_kernel(q_ref, kc_ref, vc_ref, ks_ref, vs_ref, kw_ref, vw_ref, gl_ref, ov_ref, ex_ref, o_ref):
    tq, tk = ATT_TQ, ATT_TK
    hpg = HEADS_PER_GROUP
    rows = hpg * tq
    i = pl.program_id(2)
    qb = q_ref[...]
    q4 = jnp.concatenate([qb[:, h * HEAD_DIM:(h + 1) * HEAD_DIM] for h in range(hpg)], axis=0)
    dn_t = (((1,), (1,)), ((), ()))

    t_pos = i * tq + lax.broadcasted_iota(jnp.int32, (tq, LANES), 0)
    lane = lax.broadcasted_iota(jnp.int32, (tq, LANES), 1)

    n_cmp = kc_ref.shape[0] - 1
    s = lax.dot_general(q4, kc_ref[...], dn_t, preferred_element_type=F32)
    cmask = (lane < n_cmp) & (lane * CMP_STRIDE + (CMP_BLOCK - 1) <= t_pos)
    cbias = jnp.where(cmask, 0.0, NEG_INF)
    s = (s.reshape(hpg, tq, LANES) + cbias[None]).reshape(rows, LANES)
    m = jnp.max(s, axis=-1, keepdims=True)
    e = jnp.exp(s - m).reshape(hpg, tq, LANES)
    e = jnp.where(cmask[None], e, 0.0).reshape(rows, LANES)
    den = jnp.sum(e, axis=-1, keepdims=True)
    p = e / jnp.maximum(den, 1e-30)
    o_cmp = _bdot(p.astype(BF16), vc_ref[...])
    p_sum = p[0:tq]
    for h in range(1, hpg):
        p_sum = p_sum + p[h * tq:(h + 1) * tq]
    p_hi, p_lo = _split_bf16(p_sum)
    imp = _bdot(p_hi, ov_ref[...]) + _bdot(p_lo, ov_ref[...])

    n_slc = ex_ref.shape[0] * (tk // SEL_BLOCK)
    cur = lax.shift_right_logical(t_pos, int(math.log2(SEL_BLOCK)))
    visible = (lane <= cur) & (lane < n_slc)
    forced = (lane == 0) | ((lane <= cur) & (lane > cur - N_LOCAL_SEL))
    score = jnp.where(visible, jnp.where(forced, FORCED_SCORE, imp), NEG_INF)
    rank = jnp.zeros((tq, LANES), F32)
    for k in range(n_slc):
        col = score[:, k:k + 1]
        beats = (col > score) | ((col == score) & (lane > k))
        rank = rank + jnp.where(beats, 1.0, 0.0)
    sel = jnp.where((rank < float(min(N_SELECT, n_slc))) & visible, 1.0, 0.0).astype(BF16)

    rel = (lax.broadcasted_iota(jnp.int32, (tq, tk), 0) + i * tq) - lax.broadcasted_iota(jnp.int32, (tq, tk), 1)

    def flash_step(c, carry, k_ref, v_ref, bias):
        m_p, l_p, acc_p = carry
        off = pl.multiple_of(c * tk, tk)
        kk = k_ref[pl.ds(off, tk), :]
        vv = v_ref[pl.ds(off, tk), :]
        sc = lax.dot_general(q4, kk, dn_t, preferred_element_type=F32)
        sc = (sc.reshape(hpg, tq, tk) + bias[None]).reshape(rows, tk)
        m_n = jnp.maximum(m_p, jnp.max(sc, axis=-1, keepdims=True))
        alpha = jnp.exp(m_p - m_n)
        pp = jnp.exp(sc - m_n)
        l_n = alpha * l_p + jnp.sum(pp, axis=-1, keepdims=True)
        acc_n = alpha * acc_p + _bdot(pp.astype(BF16), vv)
        return m_n, l_n, acc_n

    init = (jnp.full((rows, 1), NEG_INF, F32), jnp.zeros((rows, 1), F32), jnp.zeros((rows, HEAD_DIM), F32))

    def sel_body(c, carry):
        kmask = _bdot(sel, ex_ref[c]) > 0.5
        ok = kmask & (rel - c * tk >= 0)
        return flash_step(c, carry, ks_ref, vs_ref, jnp.where(ok, 0.0, NEG_INF))

    _, l_s, acc_s = lax.fori_loop(0, i + 1, sel_body, init)

    def win_body(c, carry):
        d = rel - c * tk
        ok = (d >= 0) & (d < WINDOW)
        return flash_step(c, carry, kw_ref, vw_ref, jnp.where(ok, 0.0, NEG_INF))

    c0 = jnp.maximum(i - WINDOW // tk, 0)
    _, l_w, acc_w = lax.fori_loop(c0, i + 1, win_body, init)

    o_sel = acc_s / l_s
    o_win = acc_w / l_w
    gates = _sigmoid(gl_ref[...])
    outs = []
    for h in range(hpg):
        sl = slice(h * tq, (h + 1) * tq)
        outs.append(gates[:, 3 * h:3 * h + 1] * o_cmp[sl]
                    + gates[:, 3 * h + 1:3 * h + 2] * o_sel[sl]
                    + gates[:, 3 * h + 2:3 * h + 3] * o_win[sl])
    o_ref[...] = jnp.concatenate(outs, axis=1).astype(o_ref.dtype)


def _attention(za, kvc, gl, ov, ex, batch, seq):
    n = za.shape[0]
    tq = ATT_TQ
    per_b = seq // tq
    g = N_KV_HEADS
    n_cb = kvc.shape[2] // g
    kv_blk = lambda col0: pl.BlockSpec((seq, HEAD_DIM), lambda b, gi, i: (b, col0 + gi))
    q_cols = Q_WIDTH // HEAD_DIM
    return pl.pallas_call(
        _attn_kernel,
        grid=(batch, g, per_b),
        in_specs=[
            pl.BlockSpec((tq, HEADS_PER_GROUP * HEAD_DIM), lambda b, gi, i: (b * per_b + i, gi)),
            pl.BlockSpec((None, None, n_cb, HEAD_DIM), lambda b, gi, i: (0, b, gi, 0)),
            pl.BlockSpec((None, None, n_cb, HEAD_DIM), lambda b, gi, i: (1, b, gi, 0)),
            kv_blk(q_cols + 2 * g), kv_blk(q_cols + 3 * g), kv_blk(q_cols + 4 * g), kv_blk(q_cols + 5 * g),
            pl.BlockSpec((None, tq, LANES), lambda b, gi, i: (gi, b * per_b + i, 0)),
            pl.BlockSpec((LANES, LANES), lambda b, gi, i: (0, 0)),
            pl.BlockSpec(ex.shape, lambda b, gi, i: (0, 0, 0)),
        ],
        out_specs=pl.BlockSpec((tq, HEADS_PER_GROUP * HEAD_DIM), lambda b, gi, i: (b * per_b + i, gi)),
        out_shape=jax.ShapeDtypeStruct((n, Q_WIDTH), BF16),
        compiler_params=_cparams(("arbitrary", "arbitrary", "arbitrary")),
        name="nsa_attention",
    )(za, kvc, kvc, za, za, za, za, gl, ov, ex)


def _rglru_kernel(x_ref, g_ref, p_ref, wg_ref, o_ref, xbuf, hc):
    tt, cb = x_ref.shape
    step = pl.program_id(2)

    @pl.when(step == 0)
    def _():
        xbuf[0:8, :] = jnp.zeros((8, cb), F32)
        hc[...] = jnp.zeros_like(hc)

    x = x_ref[...]
    xbuf[8:8 + tt, :] = x
    prm = p_ref[...]
    xc = prm[4:5] + prm[3:4] * x
    for k in range(CONV_WIDTH - 1):
        shift = CONV_WIDTH - 1 - k
        xc = xc + prm[k:k + 1] * xbuf[8 - shift:8 - shift + tt, :]
    xbuf[0:8, :] = x[tt - 8:tt]

    xcb = xc.astype(BF16)
    pre_r, pre_i = [], []
    for h in range(cb // HEAD_DIM):
        xh = xcb[:, h * HEAD_DIM:(h + 1) * HEAD_DIM]
        pre_r.append(_bdot(xh, wg_ref[0, h].astype(BF16)))
        pre_i.append(_bdot(xh, wg_ref[1, h].astype(BF16)))
    r = _sigmoid(jnp.concatenate(pre_r, axis=1) + prm[5:6])
    ig = _sigmoid(jnp.concatenate(pre_i, axis=1) + prm[6:7])
    zl = -prm[7:8]
    softplus = jnp.maximum(zl, 0.0) + jnp.log1p(jnp.exp(-jnp.abs(zl)))
    log_a = (-LRU_C) * r * softplus
    a = jnp.exp(log_a)
    b = jnp.sqrt(1.0 - a * a) * (ig * xc)

    row = lax.broadcasted_iota(jnp.int32, (tt, cb), 0)
    s = 1
    while s < tt:
        keep = row >= s
        a_sh = jnp.where(keep, pltpu.roll(a, s, 0), 1.0)
        b_sh = jnp.where(keep, pltpu.roll(b, s, 0), 0.0)
        b = a * b_sh + b
        a = a * a_sh
        s *= 2
    hcur = b + a * hc[0:1, :]
    hc[...] = jnp.broadcast_to(hcur[tt - 1:tt], hc.shape)
    o_ref[...] = (hcur * _gelu_tanh(g_ref[...])).astype(o_ref.dtype)


def _rglru(zr, prm, w_gate, batch, seq):
    n = zr.shape[0]
    tt, cb = RNN_TT, RNN_CB
    per_b = seq // tt
    n_cb = RNN_WIDTH // cb
    hpb = cb // HEAD_DIM
    return pl.pallas_call(
        _rglru_kernel,
        grid=(batch, n_cb, per_b),
        in_specs=[
            pl.BlockSpec((tt, cb), lambda b, c, t: (b * per_b + t, c)),
            pl.BlockSpec((tt, cb), lambda b, c, t: (b * per_b + t, n_cb + c)),
            pl.BlockSpec((8, cb), lambda b, c, t: (0, c)),
            pl.BlockSpec((2, hpb, HEAD_DIM, HEAD_DIM), lambda b, c, t: (0, c, 0, 0)),
        ],
        out_specs=pl.BlockSpec((tt, cb), lambda b, c, t: (b * per_b + t, c)),
        out_shape=jax.ShapeDtypeStruct((n, RNN_WIDTH), BF16),
        scratch_shapes=[pltpu.VMEM((tt + 8, cb), F32), pltpu.VMEM((8, cb), F32)],
        compiler_params=_cparams(("arbitrary", "arbitrary", "arbitrary")),
        name="rglru",
    )(zr, zr, prm, w_gate)


def _mix_kernel(a1_ref, a2_ref, w1_ref, w2_ref, m1_ref, m2_ref, o_ref):
    y1 = _bdot(a1_ref[...], w1_ref[...].astype(BF16))
    y2 = _bdot(a2_ref[...], w2_ref[...].astype(BF16))
    o_ref[...] = (_sigmoid(m1_ref[...]) * y1 + _sigmoid(m2_ref[...]) * y2).astype(o_ref.dtype)


def _mix(o_attn, o_rnn, w_branch, layer, zr):
    n, k = o_attn.shape
    tm, tn = MM_TM, 256
    m_off = 2 * RNN_WIDTH // tn
    return pl.pallas_call(
        _mix_kernel,
        grid=(n // tm, D_MODEL // tn),
        in_specs=[
            pl.BlockSpec((tm, k), lambda i, j: (i, 0)),
            pl.BlockSpec((tm, k), lambda i, j: (i, 0)),
            pl.BlockSpec((None, None, k, tn), lambda i, j: (layer, 0, 0, j)),
            pl.BlockSpec((None, None, k, tn), lambda i, j: (layer, 1, 0, j)),
            pl.BlockSpec((tm, tn), lambda i, j: (i, m_off + j)),
            pl.BlockSpec((tm, tn), lambda i, j: (i, m_off + D_MODEL // tn + j)),
        ],
        out_specs=pl.BlockSpec((tm, tn), lambda i, j: (i, j)),
        out_shape=jax.ShapeDtypeStruct((n, D_MODEL), BF16),
        compiler_params=_cparams(("arbitrary", "arbitrary")),
        name="mixer_merge",
    )(o_attn, o_rnn, w_branch, w_branch, zr, zr)


def _out_kernel(a_ref, w_ref, x_ref, g_ref, o_ref):
    o_ref[...] = x_ref[...] + g_ref[...] * _bdot(a_ref[...], w_ref[...].astype(BF16))


def _out_proj(mixed, w_out, layer, x, gate, seq):
    n, d = x.shape
    tm, tn = MM_TM, 256
    per_b = seq // tm
    return pl.pallas_call(
        _out_kernel,
        grid=(n // tm, d // tn),
        in_specs=[
            pl.BlockSpec((tm, d), lambda i, j: (i, 0)),
            pl.BlockSpec((None, d, tn), lambda i, j: (layer, 0, j)),
            pl.BlockSpec((tm, tn), lambda i, j: (i, j)),
            pl.BlockSpec((None, 1, tn), lambda i, j: (i // per_b, 0, j)),
        ],
        out_specs=pl.BlockSpec((tm, tn), lambda i, j: (i, j)),
        out_shape=jax.ShapeDtypeStruct((n, d), F32),
        compiler_params=_cparams(("arbitrary", "arbitrary")),
        name="out_proj",
    )(mixed, w_out, x, gate)


def _route_kernel(x_ref, g_ref, sc_ref, sh_ref, wr_ref, br_ref, h_ref, meta_ref, cnt_ref, run):
    tm = x_ref.shape[0]

    @pl.when(pl.program_id(0) == 0)
    def _():
        run[...] = jnp.zeros_like(run)

    h = _norm_mod(x_ref[...], g_ref[...], sc_ref[...], sh_ref[...])
    h_ref[...] = h
    h_hi, h_lo = _split_bf16(h)
    w_hi, w_lo = _split_bf16(wr_ref[...])
    logits = _bdot(h_hi, w_hi) + _bdot(h_hi, w_lo) + _bdot(h_lo, w_hi) + br_ref[...]
    lane = lax.broadcasted_iota(jnp.int32, (tm, LANES), 1)
    lane_f = lane.astype(F32)
    cur = jnp.where(lane < N_EXPERTS, logits, -jnp.inf)
    vals, idxs = [], []
    for _ in range(TOP_K):
        mx = jnp.max(cur, axis=-1, keepdims=True)
        ix = jnp.min(jnp.where(cur == mx, lane_f, float(LANES)), axis=-1, keepdims=True)
        vals.append(mx)
        idxs.append(ix)
        cur = jnp.where(lane_f == ix, -jnp.inf, cur)
    exps = [jnp.exp(v - vals[0]) for v in vals]
    den = exps[0]
    for e in exps[1:]:
        den = den + e
    hot = [lane_f == ix for ix in idxs]
    cnt = jnp.zeros((tm, LANES), F32)
    for hk in hot:
        cnt = cnt + jnp.where(hk, 1.0, 0.0)
    tri = jnp.where(lax.broadcasted_iota(jnp.int32, (tm, tm), 0) > lax.broadcasted_iota(jnp.int32, (tm, tm), 1),
                    1.0, 0.0).astype(BF16)
    pos = _bdot(tri, cnt.astype(BF16)) + run[0:1, :]
    meta = jnp.zeros((tm, LANES), F32)
    for k in range(TOP_K):
        pk = jnp.sum(jnp.where(hot[k], pos, 0.0), axis=-1, keepdims=True)
        meta = jnp.where(lane == k, idxs[k], meta)
        meta = jnp.where(lane == TOP_K + k, pk, meta)
        meta = jnp.where(lane == 2 * TOP_K + k, exps[k] / den, meta)
    meta_ref[...] = meta
    run[...] = run[...] + jnp.sum(cnt, axis=0, keepdims=True)
    cnt_ref[...] = run[...]


def _route(x, g, sc, sh, w_router, b_router, seq):
    n, d = x.shape
    tm = NORM_TM
    per_b = seq // tm
    wr = jnp.zeros((d, LANES), F32).at[:, :N_EXPERTS].set(w_router)
    br = jnp.zeros((1, LANES), F32).at[0, :N_EXPERTS].set(b_router)
    return pl.pallas_call(
        _route_kernel,
        grid=(n // tm,),
        in_specs=[
            pl.BlockSpec((tm, d), lambda i: (i, 0)),
            pl.BlockSpec((1, d), lambda i: (0, 0)),
            pl.BlockSpec((None, 1, d), lambda i: (i // per_b, 0, 0)),
            pl.BlockSpec((None, 1, d), lambda i: (i // per_b, 0, 0)),
            pl.BlockSpec((d, LANES), lambda i: (0, 0)),
            pl.BlockSpec((1, LANES), lambda i: (0, 0)),
        ],
        out_specs=[
            pl.BlockSpec((tm, d), lambda i: (i, 0)),
            pl.BlockSpec((tm, LANES), lambda i: (i, 0)),
            pl.BlockSpec((8, LANES), lambda i: (0, 0)),
        ],
        out_shape=[
            jax.ShapeDtypeStruct((n, d), F32),
            jax.ShapeDtypeStruct((n, LANES), F32),
            jax.ShapeDtypeStruct((8, LANES), F32),
        ],
        scratch_shapes=[pltpu.VMEM((8, LANES), F32)],
        compiler_params=_cparams(("arbitrary",)),
        name="moe_route",
    )(x, g.reshape(1, d), sc, sh, wr, br)


def _expert_kernel(te_ref, nt_ref, tok_ref, h_hbm, w1g_ref, w1l_ref, b1g_ref, b1l_ref, w2_ref, b2_ref, wt_ref,
                   o_ref, xbuf, sem):
    tm = o_ref.shape[0]
    j = pl.program_id(0)
    nt = nt_ref[0]
    cur = j % 2

    def row_copy(tile, r, buf):
        tok = tok_ref[tile, r]
        return pltpu.make_async_copy(h_hbm.at[pl.ds(tok, 1), :], xbuf.at[buf, pl.ds(r, 1), :], sem.at[buf])

    def start_gather(tile, buf):
        def body(r, c):
            row_copy(tile, r, buf).start()
            return c
        lax.fori_loop(0, tm, body, 0)

    @pl.when(j == 0)
    def _():
        start_gather(0, 0)

    @pl.when(j + 1 < nt)
    def _():
        start_gather(j + 1, 1 - cur)

    @pl.when(j < nt)
    def _():
        pltpu.make_async_copy(h_hbm.at[pl.ds(0, tm), :], xbuf.at[cur], sem.at[cur]).wait()
        x = xbuf[cur].astype(BF16)
        zg = _bdot(x, w1g_ref[...]) + b1g_ref[...]
        zl = _bdot(x, w1l_ref[...]) + b1l_ref[...]
        glu = jnp.minimum(zg, SWIGLU_LIMIT)
        lin = jnp.clip(zl, -SWIGLU_LIMIT, SWIGLU_LIMIT)
        act = glu * _sigmoid(SWIGLU_ALPHA * glu) * (lin + 1.0)
        y = _bdot(act.astype(BF16), w2_ref[...]) + b2_ref[...]
        o_ref[...] = y * wt_ref[...]

    @pl.when(j >= nt)
    def _():
        o_ref[...] = jnp.zeros_like(o_ref)


def _experts(h2, tile_expert, n_tiles, tok, w1g, w1l, b1g, b1l, w2, b2, wt):
    n, d = h2.shape
    tm = MOE_TM
    max_tiles = tok.shape[0]
    ff = w1g.shape[2]
    e_map = lambda j, te, nt, tk: (te[j], 0, 0)
    return pl.pallas_call(
        _expert_kernel,
        grid_spec=pltpu.PrefetchScalarGridSpec(
            num_scalar_prefetch=3,
            grid=(max_tiles,),
            in_specs=[
                pl.BlockSpec(memory_space=pl.ANY),
                pl.BlockSpec((None, d, ff), e_map),
                pl.BlockSpec((None, d, ff), e_map),
                pl.BlockSpec((None, 1, ff), e_map),
                pl.BlockSpec((None, 1, ff), e_map),
                pl.BlockSpec((None, ff, d), e_map),
                pl.BlockSpec((None, 1, d), e_map),
                pl.BlockSpec((tm, 1), lambda j, te, nt, tk: (j, 0)),
            ],
            out_specs=pl.BlockSpec((tm, d), lambda j, te, nt, tk: (j, 0)),
            scratch_shapes=[pltpu.VMEM((2, tm, d), F32), pltpu.SemaphoreType.DMA((2,))],
        ),
        out_shape=jax.ShapeDtypeStruct((max_tiles * tm, d), F32),
        compiler_params=_cparams(("arbitrary",)),
        name="moe_experts",
    )(tile_expert, n_tiles, tok, h2, w1g, w1l, b1g, b1l, w2, b2, wt)


def _combine_kernel(slot_ref, y_hbm, x_ref, g_ref, gf_ref, o_ref, ybuf, sem, *, final_norm):
    tt = x_ref.shape[0]
    i = pl.program_id(0)
    nsteps = pl.num_programs(0)
    cur = i % 2

    def start_gather(step, buf):
        def body(r, c):
            for k in range(TOP_K):
                s = slot_ref[(step * tt + r) * TOP_K + k]
                pltpu.make_async_copy(y_hbm.at[pl.ds(s, 1), :], ybuf.at[buf, k, pl.ds(r, 1), :], sem.at[buf]).start()
            return c
        lax.fori_loop(0, tt, body, 0)

    @pl.when(i == 0)
    def _():
        start_gather(0, 0)

    @pl.when(i + 1 < nsteps)
    def _():
        start_gather(i + 1, 1 - cur)

    for k in range(TOP_K):
        pltpu.make_async_copy(y_hbm.at[pl.ds(0, tt), :], ybuf.at[cur, k], sem.at[cur]).wait()
    y = ybuf[cur, 0]
    for k in range(1, TOP_K):
        y = y + ybuf[cur, k]
    out = x_ref[...] + g_ref[...] * y
    if final_norm:
        r = lax.rsqrt(jnp.mean(out * out, axis=-1, keepdims=True) + NORM_EPS)
        out = (out * r) * gf_ref[...]
    o_ref[...] = out


def _combine(slots, y_slots, x, gate, g_final, seq, final_norm):
    n, d = x.shape
    tt = COMB_TT
    per_b = seq // tt
    return pl.pallas_call(
        functools.partial(_combine_kernel, final_norm=final_norm),
        grid_spec=pltpu.PrefetchScalarGridSpec(
            num_scalar_prefetch=1,
            grid=(n // tt,),
            in_specs=[
                pl.BlockSpec(memory_space=pl.ANY),
                pl.BlockSpec((tt, d), lambda i, s: (i, 0)),
                pl.BlockSpec((None, 1, d), lambda i, s: (i // per_b, 0, 0)),
                pl.BlockSpec((1, d), lambda i, s: (0, 0)),
            ],
            out_specs=pl.BlockSpec((tt, d), lambda i, s: (i, 0)),
            scratch_shapes=[pltpu.VMEM((2, TOP_K, tt, d), F32), pltpu.SemaphoreType.DMA((2,))],
        ),
        out_shape=jax.ShapeDtypeStruct((n, d), F32),
        compiler_params=_cparams(("arbitrary",)),
        name="moe_combine",
    )(slots, y_slots, x, gate, g_final.reshape(1, d))


def _moe(x, g2, sc, sh, gate, w_router, b_router, w_exp1, b_exp1, w_exp2, b_exp2, g_final, seq, final_norm):
    n, d = x.shape
    tm = MOE_TM
    max_tiles = n * TOP_K // tm + N_EXPERTS
    h2, meta, cnt = _route(x, g2, sc, sh, w_router, b_router, seq)

    idx = meta[:, 0:TOP_K].astype(jnp.int32)
    pos = meta[:, TOP_K:2 * TOP_K].astype(jnp.int32)
    wts = meta[:, 2 * TOP_K:3 * TOP_K]
    counts = cnt[0, :N_EXPERTS].astype(jnp.int32)
    tiles_e = (counts + tm - 1) // tm
    tile_end = jnp.cumsum(tiles_e)
    offsets = (tile_end - tiles_e) * tm
    n_tiles = tile_end[-1:]
    tile_ids = jnp.arange(max_tiles, dtype=jnp.int32)
    tile_expert = jnp.minimum(jnp.sum((tile_end[None, :] <= tile_ids[:, None]).astype(jnp.int32), axis=1),
                              N_EXPERTS - 1)
    slot = offsets[idx] + pos
    tok_ids = jnp.broadcast_to(jnp.arange(n, dtype=jnp.int32)[:, None], slot.shape)
    tok = jnp.zeros((max_tiles * tm,), jnp.int32).at[slot.reshape(-1)].set(tok_ids.reshape(-1))
    wt = jnp.zeros((max_tiles * tm,), F32).at[slot.reshape(-1)].set(wts.reshape(-1))

    w1g = w_exp1[:, :, 0::2].astype(BF16)
    w1l = w_exp1[:, :, 1::2].astype(BF16)
    b1g = b_exp1[:, None, 0::2]
    b1l = b_exp1[:, None, 1::2]
    w2 = w_exp2.astype(BF16)
    y_slots = _experts(h2, tile_expert, n_tiles, tok.reshape(max_tiles, tm), w1g, w1l, b1g, b1l, w2,
                       b_exp2[:, None, :], wt.reshape(-1, 1))
    return _combine(slot.reshape(-1), y_slots, x, gate, g_final, seq, final_norm)


def _overlap_matrix(seq):
    n_cmp = (seq - CMP_BLOCK) // CMP_STRIDE + 1
    n_slc = seq // SEL_BLOCK
    cmp_start = np.arange(n_cmp) * CMP_STRIDE
    slc_start = np.arange(n_slc) * SEL_BLOCK
    ov = np.clip(np.minimum(cmp_start[:, None] + CMP_BLOCK, slc_start[None, :] + SEL_BLOCK)
                 - np.maximum(cmp_start[:, None], slc_start[None, :]), 0, None) / CMP_BLOCK
    out = np.zeros((LANES, LANES), np.float32)
    out[:n_cmp, :n_slc] = ov
    return jnp.asarray(out, BF16)


def _expand_matrix(seq):
    n_chunks = seq // ATT_TK
    key = np.arange(seq).reshape(n_chunks, 1, ATT_TK)
    blk = np.arange(LANES).reshape(1, LANES, 1)
    return jnp.asarray((key // SEL_BLOCK == blk).astype(np.float32), BF16)


def _cmp_rope_tables(seq):
    n_cmp = (seq - CMP_BLOCK) // CMP_STRIDE + 1
    half = ROPE_DIM // 2
    inv_freq = jnp.power(ROPE_THETA, -jnp.arange(half, dtype=F32) * 2.0 / ROPE_DIM)
    pos = jnp.arange(n_cmp + 1, dtype=F32) * CMP_STRIDE + (CMP_BLOCK - 1)
    ang = pos[:, None] * inv_freq[None, :]
    cos, sin = jnp.cos(ang), jnp.sin(ang)
    ones = jnp.ones((n_cmp + 1, HEAD_DIM - ROPE_DIM), F32)
    c_k = jnp.concatenate([cos, cos, ones], axis=1)
    s_k = jnp.concatenate([-sin, sin, 0.0 * ones], axis=1)
    return jnp.stack([c_k, jnp.ones_like(c_k)]), jnp.stack([s_k, jnp.zeros_like(s_k)])


def kernel(x, c, w_ada, b_ada, ada_table, g_norm1, w_in, w_cmp1, w_cmp2, cmp_pos, conv_w, conv_b,
           w_lru_gate, b_lru_gate, lru_lambda, w_branch, w_out, g_norm2, w_router, b_router,
           w_exp1, b_exp1, w_exp2, b_exp2, g_final):
    batch, seq, d = x.shape
    depth = w_in.shape[0]
    n = batch * seq
    g = N_KV_HEADS
    assert d == D_MODEL and seq % MM_TM == 0 and seq % ATT_TQ == 0 and n % COMB_TT == 0

    mod = _ada_mod(c, w_ada, b_ada, ada_table)
    cos_t, sin_t = _rope_tables(seq)
    cos_c, sin_c = _cmp_rope_tables(seq)
    ov = _overlap_matrix(seq)
    ex = _expand_matrix(seq)

    xf = x.reshape(n, d)
    for l in range(depth):
        m = [mod[l, :batch, j * d:(j + 1) * d].reshape(batch, 1, d) for j in range(N_ADA)]
        shift1, scale1, gate1, shift2, scale2, gate2 = m

        h = _norm_call(xf, g_norm1[l], scale1, shift1, seq)
        za = _proj_a(h, w_in, l, cos_t, sin_t, seq)
        zg = _proj_gate(h, w_in, l)
        zr = _proj_r(h, w_in[l, :, SEG_R0:].astype(BF16))

        kv = za[:, Q_WIDTH:Q_WIDTH + 2 * KV_WIDTH].reshape(batch, seq // CMP_STRIDE, CMP_STRIDE, 2, g, HEAD_DIM)
        z2 = kv.transpose(3, 0, 4, 1, 2, 5).reshape(2, batch, g * (seq // CMP_STRIDE), CMP_STRIDE * HEAD_DIM)
        kvc = _compress(z2, w_cmp1, w_cmp2, cmp_pos, l, cos_c, sin_c)
        gl = zg[:, :GATE_W].reshape(n, g, 3 * HEADS_PER_GROUP).transpose(1, 0, 2)
        gl = jnp.pad(gl, ((0, 0), (0, 0), (0, LANES - 3 * HEADS_PER_GROUP)))
        o_attn = _attention(za, kvc, gl, ov, ex, batch, seq)

        prm = jnp.concatenate([conv_w[l], conv_b[l][None], b_lru_gate[l], lru_lambda[l][None]], axis=0)
        o_rnn = _rglru(zr, prm, w_lru_gate[l], batch, seq)

        mixed = _mix(o_attn, o_rnn, w_branch, l, zr)
        xf = _out_proj(mixed, w_out, l, xf, gate1, seq)

        xf = _moe(xf, g_norm2[l], scale2, shift2, gate2, w_router[l], b_router[l], w_exp1[l], b_exp1[l],
                  w_exp2[l], b_exp2[l], g_final, seq, final_norm=(l == depth - 1))
    return xf.reshape(batch, seq, d)
```

```python
import functools
import math

import numpy as np
import jax
import jax.numpy as jnp
from jax import lax
from jax.experimental import pallas as pl
from jax.experimental.pallas import tpu as pltpu

F32 = jnp.float32
BF16 = jnp.bfloat16

D_MODEL = 4096
HEAD_DIM = 128
N_Q_HEADS = 16
N_KV_HEADS = 4
HEADS_PER_GROUP = 4
Q_WIDTH = N_Q_HEADS * HEAD_DIM
KV_WIDTH = N_KV_HEADS * HEAD_DIM
ATTN_SCALE = HEAD_DIM ** -0.5
ROPE_THETA = 500000.0
ROPE_DIM = 32
CMP_BLOCK = 32
CMP_STRIDE = 16
CMP_HIDDEN = 256
SEL_BLOCK = 64
N_SELECT = 16
N_LOCAL_SEL = 2
WINDOW = 512
RNN_WIDTH = 2048
RNN_HEADS = 16
CONV_WIDTH = 4
LRU_C = 8.0
N_EXPERTS = 32
TOP_K = 4
EXPERT_FF = 512
SWIGLU_LIMIT = 7.0
SWIGLU_ALPHA = 1.702
N_ADA = 6
NORM_EPS = 1e-6
NEG_INF = -1e30
FORCED_SCORE = 1e6

SEG_A = Q_WIDTH + 6 * KV_WIDTH
GATE_W = 3 * N_Q_HEADS
SEG_R0 = SEG_A + GATE_W
SEG_R = 2 * RNN_WIDTH + 2 * D_MODEL

LANES = 128
VMEM_LIMIT_MB = 56
MM_TM = 1024
MM_TN = 512
ATT_TQ = 256
ATT_TK = 256
RNN_TT = 256
RNN_CB = 256
NORM_TM = 256
MOE_TM = 256
COMB_TT = 128


def _cparams(sem):
    return pltpu.CompilerParams(dimension_semantics=sem, vmem_limit_bytes=VMEM_LIMIT_MB << 20)


def _sigmoid(x):
    return 1.0 / (1.0 + jnp.exp(-x))


def _gelu_tanh(x):
    return 0.5 * x * (1.0 + jnp.tanh(0.7978845608028654 * (x + 0.044715 * (x * x * x))))


def _bdot(a, b):
    return jnp.dot(a, b, preferred_element_type=F32)


def _split_bf16(x):
    hi = x.astype(BF16)
    lo = (x - hi.astype(F32)).astype(BF16)
    return hi, lo


def _ada_kernel(c_ref, w_ref, b_ref, t_ref, o_ref):
    c = c_ref[...]
    s = (c * _sigmoid(c)).astype(BF16)
    emb = _bdot(s, w_ref[...].astype(BF16)) + b_ref[...]
    for l in range(o_ref.shape[0]):
        o_ref[l] = emb + t_ref[l]


def _ada_mod(c, w_ada, b_ada, ada_table):
    depth = ada_table.shape[0]
    b = c.shape[0]
    n = w_ada.shape[1]
    c8 = jnp.zeros((8, D_MODEL), F32).at[:b].set(c)
    tn = 512
    return pl.pallas_call(
        _ada_kernel,
        grid=(n // tn,),
        in_specs=[
            pl.BlockSpec((8, D_MODEL), lambda j: (0, 0)),
            pl.BlockSpec((D_MODEL, tn), lambda j: (0, j)),
            pl.BlockSpec((1, tn), lambda j: (0, j)),
            pl.BlockSpec((depth, 1, tn), lambda j: (0, 0, j)),
        ],
        out_specs=pl.BlockSpec((depth, 8, tn), lambda j: (0, 0, j)),
        out_shape=jax.ShapeDtypeStruct((depth, 8, n), F32),
        compiler_params=_cparams(("arbitrary",)),
        name="ada_mod",
    )(c8, w_ada, b_ada.reshape(1, n), ada_table.reshape(depth, 1, n))


def _norm_mod(x, g, sc, sh):
    r = lax.rsqrt(jnp.mean(x * x, axis=-1, keepdims=True) + NORM_EPS)
    return ((x * r) * g) * (1.0 + sc) + sh


def _norm_kernel(x_ref, g_ref, sc_ref, sh_ref, o_ref):
    o_ref[...] = _norm_mod(x_ref[...], g_ref[...], sc_ref[...], sh_ref[...]).astype(o_ref.dtype)


def _norm_call(x, g, sc, sh, seq):
    n, d = x.shape
    tm = NORM_TM
    per_b = seq // tm
    return pl.pallas_call(
        _norm_kernel,
        grid=(n // tm,),
        in_specs=[
            pl.BlockSpec((tm, d), lambda i: (i, 0)),
            pl.BlockSpec((1, d), lambda i: (0, 0)),
            pl.BlockSpec((None, 1, d), lambda i: (i // per_b, 0, 0)),
            pl.BlockSpec((None, 1, d), lambda i: (i // per_b, 0, 0)),
        ],
        out_specs=pl.BlockSpec((tm, d), lambda i: (i, 0)),
        out_shape=jax.ShapeDtypeStruct((n, d), BF16),
        compiler_params=_cparams(("arbitrary",)),
        name="norm_mod",
    )(x, g.reshape(1, d), sc, sh)


def _rope_tables(seq):
    half = ROPE_DIM // 2
    inv_freq = jnp.power(ROPE_THETA, -jnp.arange(half, dtype=F32) * 2.0 / ROPE_DIM)
    ang = jnp.arange(seq, dtype=F32)[:, None] * inv_freq[None, :]
    cos, sin = jnp.cos(ang), jnp.sin(ang)
    ones = jnp.ones((seq, HEAD_DIM - ROPE_DIM), F32)
    c_k = jnp.concatenate([cos, cos, ones], axis=1)
    s_k = jnp.concatenate([-sin, sin, 0.0 * ones], axis=1)
    c_id = jnp.ones((seq, HEAD_DIM), F32)
    s_id = jnp.zeros((seq, HEAD_DIM), F32)
    return (jnp.stack([c_k * ATTN_SCALE, c_k, c_id]), jnp.stack([s_k * ATTN_SCALE, s_k, s_id]))


def _rope_apply(x, cos, sin):
    heads = x.shape[1] // HEAD_DIM
    lane = lax.broadcasted_iota(jnp.int32, x.shape, 1) & (HEAD_DIM - 1)
    half = ROPE_DIM // 2
    fwd = pltpu.roll(x, half, 1)
    bwd = pltpu.roll(x, x.shape[1] - half, 1)
    sw = jnp.where(lane < half, bwd, fwd)
    if heads > 1:
        cos = jnp.concatenate([cos] * heads, axis=1)
        sin = jnp.concatenate([sin] * heads, axis=1)
    return x * cos + sw * sin


def _proj_rope_kernel(a_ref, w_ref, cos_ref, sin_ref, o_ref):
    acc = _bdot(a_ref[...], w_ref[...].astype(BF16))
    o_ref[...] = _rope_apply(acc, cos_ref[...], sin_ref[...]).astype(o_ref.dtype)


def _proj_kernel(a_ref, w_ref, o_ref):
    o_ref[...] = _bdot(a_ref[...], w_ref[...].astype(BF16)).astype(o_ref.dtype)


def _proj_a(h, w_in, layer, cos_t, sin_t, seq):
    n, d = h.shape
    tm, tn = MM_TM, MM_TN
    per_b = seq // tm

    def variant(j):
        return jnp.where(j < 4, 0, jnp.where((j == 6) | (j == 8), 1, 2))

    return pl.pallas_call(
        _proj_rope_kernel,
        grid=(n // tm, SEG_A // tn),
        in_specs=[
            pl.BlockSpec((tm, d), lambda i, j: (i, 0)),
            pl.BlockSpec((None, d, tn), lambda i, j: (layer, 0, j)),
            pl.BlockSpec((None, tm, HEAD_DIM), lambda i, j: (variant(j), i % per_b, 0)),
            pl.BlockSpec((None, tm, HEAD_DIM), lambda i, j: (variant(j), i % per_b, 0)),
        ],
        out_specs=pl.BlockSpec((tm, tn), lambda i, j: (i, j)),
        out_shape=jax.ShapeDtypeStruct((n, SEG_A), BF16),
        compiler_params=_cparams(("arbitrary", "arbitrary")),
        name="proj_qkv",
    )(h, w_in, cos_t, sin_t)


def _proj_gate(h, w_in, layer):
    n, d = h.shape
    tm = MM_TM
    return pl.pallas_call(
        _proj_kernel,
        grid=(n // tm,),
        in_specs=[
            pl.BlockSpec((tm, d), lambda i: (i, 0)),
            pl.BlockSpec((None, d, LANES), lambda i: (layer, 0, SEG_A // LANES)),
        ],
        out_specs=pl.BlockSpec((tm, LANES), lambda i: (i, 0)),
        out_shape=jax.ShapeDtypeStruct((n, LANES), F32),
        compiler_params=_cparams(("arbitrary",)),
        name="proj_gate",
    )(h, w_in)


def _proj_shift_kernel(a_ref, wa_ref, wb_ref, o_ref):
    tn = wa_ref.shape[1]
    w = jnp.concatenate([wa_ref[...], wb_ref[...]], axis=1)[:, GATE_W:GATE_W + tn]
    o_ref[...] = _bdot(a_ref[...], w.astype(BF16))


def _proj_r(h, w_in, layer):
    n, d = h.shape
    tm, tn = MM_TM, 256
    return pl.pallas_call(
        _proj_shift_kernel,
        grid=(n // tm, SEG_R // tn),
        in_specs=[
            pl.BlockSpec((tm, d), lambda i, j: (i, 0)),
            pl.BlockSpec((None, d, tn), lambda i, j: (layer, 0, SEG_A // tn + j)),
            pl.BlockSpec((None, d, LANES), lambda i, j: (layer, 0, (SEG_A + tn * (j + 1)) // LANES)),
        ],
        out_specs=pl.BlockSpec((tm, tn), lambda i, j: (i, j)),
        out_shape=jax.ShapeDtypeStruct((n, SEG_R), F32),
        compiler_params=_cparams(("arbitrary", "arbitrary")),
        name="proj_rnn_mix",
    )(h, w_in, w_in)


def _cmp_kernel(z_ref, w1_ref, w2_ref, pos_ref, cos_ref, sin_ref, o_ref):
    z = z_ref[...]
    w1 = w1_ref[...].astype(BF16)
    half = w1.shape[0] // 2
    a = _bdot(z, w1[:half])
    b = _bdot(z, w1[half:])
    pos = jnp.broadcast_to(pos_ref[...], (8, pos_ref.shape[1])).astype(BF16)
    bias = _bdot(pos, w1)[0:1]
    y = a + pltpu.roll(b, b.shape[0] - 1, 0) + bias
    hid = _gelu_tanh(y).astype(BF16)
    out = _bdot(hid, w2_ref[...].astype(BF16))
    groups = out.shape[0] // cos_ref.shape[0]
    cos = jnp.concatenate([cos_ref[...]] * groups, axis=0)
    sin = jnp.concatenate([sin_ref[...]] * groups, axis=0)
    o_ref[...] = _rope_apply(out, cos, sin).astype(o_ref.dtype)


def _compress(z2, w_cmp1, w_cmp2, cmp_pos, layer, cos_c, sin_c):
    _, b, rows, width = z2.shape
    n_grp = cos_c.shape[1]
    return pl.pallas_call(
        _cmp_kernel,
        grid=(2, b),
        in_specs=[
            pl.BlockSpec((None, None, rows, width), lambda j, bi: (j, bi, 0, 0)),
            pl.BlockSpec((None, None, 2 * width, CMP_HIDDEN), lambda j, bi: (layer, j, 0, 0)),
            pl.BlockSpec((None, None, CMP_HIDDEN, HEAD_DIM), lambda j, bi: (layer, j, 0, 0)),
            pl.BlockSpec((None, None, 1, 2 * width), lambda j, bi: (layer, j, 0, 0)),
            pl.BlockSpec((None, n_grp, HEAD_DIM), lambda j, bi: (j, 0, 0)),
            pl.BlockSpec((None, n_grp, HEAD_DIM), lambda j, bi: (j, 0, 0)),
        ],
        out_specs=pl.BlockSpec((None, None, rows, HEAD_DIM), lambda j, bi: (j, bi, 0, 0)),
        out_shape=jax.ShapeDtypeStruct((2, b, rows, HEAD_DIM), BF16),
        compiler_params=_cparams(("arbitrary", "arbitrary")),
        name="nsa_compress",
    )(z2, w_cmp1, w_cmp2, cmp_pos.reshape(cmp_pos.shape[0], 2, 1, CMP_BLOCK * HEAD_DIM), cos_c, sin_c)


def _attn_kernel(q_ref, kc_ref, vc_ref, ks_ref, vs_ref, kw_ref, vw_ref, gl_ref, ov_ref, ex_ref, o_ref):
    tq, tk = ATT_TQ, ATT_TK
    hpg = HEADS_PER_GROUP
    rows = hpg * tq
    i = pl.program_id(2)
    qb = q_ref[...]
    q4 = jnp.concatenate([qb[:, h * HEAD_DIM:(h + 1) * HEAD_DIM] for h in range(hpg)], axis=0)
    dn_t = (((1,), (1,)), ((), ()))

    t_pos = i * tq + lax.broadcasted_iota(jnp.int32, (tq, LANES), 0)
    lane = lax.broadcasted_iota(jnp.int32, (tq, LANES), 1)

    n_cmp = kc_ref.shape[0] - 1
    s = lax.dot_general(q4, kc_ref[...], dn_t, preferred_element_type=F32)
    cmask = (lane < n_cmp) & (lane * CMP_STRIDE + (CMP_BLOCK - 1) <= t_pos)
    cbias = jnp.where(cmask, 0.0, NEG_INF)
    s = (s.reshape(hpg, tq, LANES) + cbias[None]).reshape(rows, LANES)
    m = jnp.max(s, axis=-1, keepdims=True)
    e = jnp.exp(s - m).reshape(hpg, tq, LANES)
    e = jnp.where(cmask[None], e, 0.0).reshape(rows, LANES)
    den = jnp.sum(e, axis=-1, keepdims=True)
    p = e / jnp.maximum(den, 1e-30)
    o_cmp = _bdot(p.astype(BF16), vc_ref[...])
    p_sum = p[0:tq]
    for h in range(1, hpg):
        p_sum = p_sum + p[h * tq:(h + 1) * tq]
    p_hi, p_lo = _split_bf16(p_sum)
    imp = _bdot(p_hi, ov_ref[...]) + _bdot(p_lo, ov_ref[...])

    n_slc = ex_ref.shape[0] * (tk // SEL_BLOCK)
    cur = lax.shift_right_logical(t_pos, int(math.log2(SEL_BLOCK)))
    visible = (lane <= cur) & (lane < n_slc)
    forced = (lane == 0) | ((lane <= cur) & (lane > cur - N_LOCAL_SEL))
    score = jnp.where(visible, jnp.where(forced, FORCED_SCORE, imp), NEG_INF)
    score_t = jnp.transpose(score)[:n_slc]
    blk = lax.broadcasted_iota(jnp.int32, score_t.shape, 0)
    rank = jnp.zeros(score_t.shape, F32)
    for k in range(n_slc):
        row = score_t[k:k + 1, :]
        beats = (row > score_t) | ((row == score_t) & (blk > k))
        rank = rank + jnp.where(beats, 1.0, 0.0)
    top_t = jnp.where(rank < float(min(N_SELECT, n_slc)), 1.0, 0.0)
    top = jnp.transpose(jnp.concatenate([top_t, jnp.zeros((LANES - n_slc, tq), F32)], axis=0))
    sel = jnp.where((top > 0.5) & visible, 1.0, 0.0).astype(BF16)

    rel = (lax.broadcasted_iota(jnp.int32, (tq, tk), 0) + i * tq) - lax.broadcasted_iota(jnp.int32, (tq, tk), 1)

    def flash_step(c, carry, k_ref, v_ref, bias):
        m_p, l_p, acc_p = carry
        off = pl.multiple_of(c * tk, tk)
        kk = k_ref[pl.ds(off, tk), :]
        vv = v_ref[pl.ds(off, tk), :]
        sc = lax.dot_general(q4, kk, dn_t, preferred_element_type=F32)
        sc = (sc.reshape(hpg, tq, tk) + bias[None]).reshape(rows, tk)
        m_n = jnp.maximum(m_p, jnp.max(sc, axis=-1, keepdims=True))
        alpha = jnp.exp(m_p - m_n)
        pp = jnp.exp(sc - m_n)
        l_n = alpha * l_p + jnp.sum(pp, axis=-1, keepdims=True)
        acc_n = alpha * acc_p + _bdot(pp.astype(BF16), vv)
        return m_n, l_n, acc_n

    init = (jnp.full((rows, 1), NEG_INF, F32), jnp.zeros((rows, 1), F32), jnp.zeros((rows, HEAD_DIM), F32))

    def sel_body(c, carry):
        kmask = _bdot(sel, ex_ref[c]) > 0.5
        ok = kmask & (rel - c * tk >= 0)
        return flash_step(c, carry, ks_ref, vs_ref, jnp.where(ok, 0.0, NEG_INF))

    _, l_s, acc_s = lax.fori_loop(0, i + 1, sel_body, init)

    def win_body(c, carry):
        d = rel - c * tk
        ok = (d >= 0) & (d < WINDOW)
        return flash_step(c, carry, kw_ref, vw_ref, jnp.where(ok, 0.0, NEG_INF))

    c0 = jnp.maximum(i - WINDOW // tk, 0)
    _, l_w, acc_w = lax.fori_loop(c0, i + 1, win_body, init)

    o_sel = acc_s / l_s
    o_win = acc_w / l_w
    gates = _sigmoid(gl_ref[...])
    outs = []
    for h in range(hpg):
        sl = slice(h * tq, (h + 1) * tq)
        outs.append(gates[:, 3 * h:3 * h + 1] * o_cmp[sl]
                    + gates[:, 3 * h + 1:3 * h + 2] * o_sel[sl]
                    + gates[:, 3 * h + 2:3 * h + 3] * o_win[sl])
    o_ref[...] = jnp.concatenate(outs, axis=1).astype(o_ref.dtype)


def _attention(za, kvc, gl, ov, ex, batch, seq):
    n = za.shape[0]
    tq = ATT_TQ
    per_b = seq // tq
    g = N_KV_HEADS
    n_cb = kvc.shape[2] // g
    kv_blk = lambda col0: pl.BlockSpec((seq, HEAD_DIM), lambda b, gi, i: (b, col0 + gi))
    q_cols = Q_WIDTH // HEAD_DIM
    return pl.pallas_call(
        _attn_kernel,
        grid=(batch, g, per_b),
        in_specs=[
            pl.BlockSpec((tq, HEADS_PER_GROUP * HEAD_DIM), lambda b, gi, i: (b * per_b + i, gi)),
            pl.BlockSpec((None, None, n_cb, HEAD_DIM), lambda b, gi, i: (0, b, gi, 0)),
            pl.BlockSpec((None, None, n_cb, HEAD_DIM), lambda b, gi, i: (1, b, gi, 0)),
            kv_blk(q_cols + 2 * g), kv_blk(q_cols + 3 * g), kv_blk(q_cols + 4 * g), kv_blk(q_cols + 5 * g),
            pl.BlockSpec((None, tq, LANES), lambda b, gi, i: (gi, b * per_b + i, 0)),
            pl.BlockSpec((LANES, LANES), lambda b, gi, i: (0, 0)),
            pl.BlockSpec(ex.shape, lambda b, gi, i: (0, 0, 0)),
        ],
        out_specs=pl.BlockSpec((tq, HEADS_PER_GROUP * HEAD_DIM), lambda b, gi, i: (b * per_b + i, gi)),
        out_shape=jax.ShapeDtypeStruct((n, Q_WIDTH), BF16),
        compiler_params=_cparams(("arbitrary", "arbitrary", "arbitrary")),
        name="nsa_attention",
    )(za, kvc, kvc, za, za, za, za, gl, ov, ex)


def _rglru_kernel(x_ref, g_ref, p_ref, wg_ref, o_ref, xbuf, hc):
    tt, cb = x_ref.shape
    step = pl.program_id(2)

    @pl.when(step == 0)
    def _():
        xbuf[0:8, :] = jnp.zeros((8, cb), F32)
        hc[...] = jnp.zeros_like(hc)

    x = x_ref[...]
    xbuf[8:8 + tt, :] = x
    prm = p_ref[...]
    xc = prm[4:5] + prm[3:4] * x
    for k in range(CONV_WIDTH - 1):
        shift = CONV_WIDTH - 1 - k
        xc = xc + prm[k:k + 1] * xbuf[8 - shift:8 - shift + tt, :]
    xbuf[0:8, :] = x[tt - 8:tt]

    xcb = xc.astype(BF16)
    pre_r, pre_i = [], []
    for h in range(cb // HEAD_DIM):
        xh = xcb[:, h * HEAD_DIM:(h + 1) * HEAD_DIM]
        pre_r.append(_bdot(xh, wg_ref[0, h].astype(BF16)))
        pre_i.append(_bdot(xh, wg_ref[1, h].astype(BF16)))
    r = _sigmoid(jnp.concatenate(pre_r, axis=1) + prm[5:6])
    ig = _sigmoid(jnp.concatenate(pre_i, axis=1) + prm[6:7])
    zl = -prm[7:8]
    softplus = jnp.maximum(zl, 0.0) + jnp.log1p(jnp.exp(-jnp.abs(zl)))
    log_a = (-LRU_C) * r * softplus
    a = jnp.exp(log_a)
    b = jnp.sqrt(1.0 - a * a) * (ig * xc)

    row = lax.broadcasted_iota(jnp.int32, (tt, cb), 0)
    s = 1
    while s < tt:
        keep = row >= s
        a_sh = jnp.where(keep, pltpu.roll(a, s, 0), 1.0)
        b_sh = jnp.where(keep, pltpu.roll(b, s, 0), 0.0)
        b = a * b_sh + b
        a = a * a_sh
        s *= 2
    hcur = b + a * hc[0:1, :]
    hc[...] = jnp.broadcast_to(hcur[tt - 1:tt], hc.shape)
    o_ref[...] = (hcur * _gelu_tanh(g_ref[...])).astype(o_ref.dtype)


def _rglru(zr, prm, w_gate, batch, seq):
    n = zr.shape[0]
    tt, cb = RNN_TT, RNN_CB
    per_b = seq // tt
    n_cb = RNN_WIDTH // cb
    hpb = cb // HEAD_DIM
    return pl.pallas_call(
        _rglru_kernel,
        grid=(batch, n_cb, per_b),
        in_specs=[
            pl.BlockSpec((tt, cb), lambda b, c, t: (b * per_b + t, c)),
            pl.BlockSpec((tt, cb), lambda b, c, t: (b * per_b + t, n_cb + c)),
            pl.BlockSpec((8, cb), lambda b, c, t: (0, c)),
            pl.BlockSpec((2, hpb, HEAD_DIM, HEAD_DIM), lambda b, c, t: (0, c, 0, 0)),
        ],
        out_specs=pl.BlockSpec((tt, cb), lambda b, c, t: (b * per_b + t, c)),
        out_shape=jax.ShapeDtypeStruct((n, RNN_WIDTH), BF16),
        scratch_shapes=[pltpu.VMEM((tt + 8, cb), F32), pltpu.VMEM((8, cb), F32)],
        compiler_params=_cparams(("arbitrary", "arbitrary", "arbitrary")),
        name="rglru",
    )(zr, zr, prm, w_gate)


def _mix_kernel(a1_ref, a2_ref, w1_ref, w2_ref, m1_ref, m2_ref, o_ref):
    y1 = _bdot(a1_ref[...], w1_ref[...].astype(BF16))
    y2 = _bdot(a2_ref[...], w2_ref[...].astype(BF16))
    o_ref[...] = (_sigmoid(m1_ref[...]) * y1 + _sigmoid(m2_ref[...]) * y2).astype(o_ref.dtype)


def _mix(o_attn, o_rnn, w_branch, layer, zr):
    n, k = o_attn.shape
    tm, tn = MM_TM, 256
    m_off = 2 * RNN_WIDTH // tn
    return pl.pallas_call(
        _mix_kernel,
        grid=(n // tm, D_MODEL // tn),
        in_specs=[
            pl.BlockSpec((tm, k), lambda i, j: (i, 0)),
            pl.BlockSpec((tm, k), lambda i, j: (i, 0)),
            pl.BlockSpec((None, None, k, tn), lambda i, j: (layer, 0, 0, j)),
            pl.BlockSpec((None, None, k, tn), lambda i, j: (layer, 1, 0, j)),
            pl.BlockSpec((tm, tn), lambda i, j: (i, m_off + j)),
            pl.BlockSpec((tm, tn), lambda i, j: (i, m_off + D_MODEL // tn + j)),
        ],
        out_specs=pl.BlockSpec((tm, tn), lambda i, j: (i, j)),
        out_shape=jax.ShapeDtypeStruct((n, D_MODEL), BF16),
        compiler_params=_cparams(("arbitrary", "arbitrary")),
        name="mixer_merge",
    )(o_attn, o_rnn, w_branch, w_branch, zr, zr)


def _out_kernel(a_ref, w_ref, x_ref, g_ref, o_ref):
    o_ref[...] = x_ref[...] + g_ref[...] * _bdot(a_ref[...], w_ref[...].astype(BF16))


def _out_proj(mixed, w_out, layer, x, gate, seq):
    n, d = x.shape
    tm, tn = MM_TM, 256
    per_b = seq // tm
    return pl.pallas_call(
        _out_kernel,
        grid=(n // tm, d // tn),
        in_specs=[
            pl.BlockSpec((tm, d), lambda i, j: (i, 0)),
            pl.BlockSpec((None, d, tn), lambda i, j: (layer, 0, j)),
            pl.BlockSpec((tm, tn), lambda i, j: (i, j)),
            pl.BlockSpec((None, 1, tn), lambda i, j: (i // per_b, 0, j)),
        ],
        out_specs=pl.BlockSpec((tm, tn), lambda i, j: (i, j)),
        out_shape=jax.ShapeDtypeStruct((n, d), F32),
        compiler_params=_cparams(("arbitrary", "arbitrary")),
        name="out_proj",
    )(mixed, w_out, x, gate)


def _route_kernel(x_ref, g_ref, sc_ref, sh_ref, wr_ref, br_ref, h_ref, meta_ref, cnt_ref, run):
    tm = x_ref.shape[0]

    @pl.when(pl.program_id(0) == 0)
    def _():
        run[...] = jnp.zeros_like(run)

    h = _norm_mod(x_ref[...], g_ref[...], sc_ref[...], sh_ref[...])
    h_ref[...] = h
    h_hi, h_lo = _split_bf16(h)
    w_hi, w_lo = _split_bf16(wr_ref[...])
    logits = _bdot(h_hi, w_hi) + _bdot(h_hi, w_lo) + _bdot(h_lo, w_hi) + br_ref[...]
    lane = lax.broadcasted_iota(jnp.int32, (tm, LANES), 1)
    lane_f = lane.astype(F32)
    cur = jnp.where(lane < N_EXPERTS, logits, -jnp.inf)
    vals, idxs = [], []
    for _ in range(TOP_K):
        mx = jnp.max(cur, axis=-1, keepdims=True)
        ix = jnp.min(jnp.where(cur == mx, lane_f, float(LANES)), axis=-1, keepdims=True)
        vals.append(mx)
        idxs.append(ix)
        cur = jnp.where(lane_f == ix, -jnp.inf, cur)
    exps = [jnp.exp(v - vals[0]) for v in vals]
    den = exps[0]
    for e in exps[1:]:
        den = den + e
    hot = [lane_f == ix for ix in idxs]
    cnt = jnp.zeros((tm, LANES), F32)
    for hk in hot:
        cnt = cnt + jnp.where(hk, 1.0, 0.0)
    tri = jnp.where(lax.broadcasted_iota(jnp.int32, (tm, tm), 0) > lax.broadcasted_iota(jnp.int32, (tm, tm), 1),
                    1.0, 0.0).astype(BF16)
    pos = _bdot(tri, cnt.astype(BF16)) + run[0:1, :]
    meta = jnp.zeros((tm, LANES), F32)
    for k in range(TOP_K):
        pk = jnp.sum(jnp.where(hot[k], pos, 0.0), axis=-1, keepdims=True)
        meta = jnp.where(lane == k, idxs[k], meta)
        meta = jnp.where(lane == TOP_K + k, pk, meta)
        meta = jnp.where(lane == 2 * TOP_K + k, exps[k] / den, meta)
    meta_ref[...] = meta
    run[...] = run[...] + jnp.sum(cnt, axis=0, keepdims=True)
    cnt_ref[...] = run[...]


def _route(x, g, sc, sh, w_router, b_router, seq):
    n, d = x.shape
    tm = NORM_TM
    per_b = seq // tm
    wr = jnp.zeros((d, LANES), F32).at[:, :N_EXPERTS].set(w_router)
    br = jnp.zeros((1, LANES), F32).at[0, :N_EXPERTS].set(b_router)
    return pl.pallas_call(
        _route_kernel,
        grid=(n // tm,),
        in_specs=[
            pl.BlockSpec((tm, d), lambda i: (i, 0)),
            pl.BlockSpec((1, d), lambda i: (0, 0)),
            pl.BlockSpec((None, 1, d), lambda i: (i // per_b, 0, 0)),
            pl.BlockSpec((None, 1, d), lambda i: (i // per_b, 0, 0)),
            pl.BlockSpec((d, LANES), lambda i: (0, 0)),
            pl.BlockSpec((1, LANES), lambda i: (0, 0)),
        ],
        out_specs=[
            pl.BlockSpec((tm, d), lambda i: (i, 0)),
            pl.BlockSpec((tm, LANES), lambda i: (i, 0)),
            pl.BlockSpec((8, LANES), lambda i: (0, 0)),
        ],
        out_shape=[
            jax.ShapeDtypeStruct((n, d), F32),
            jax.ShapeDtypeStruct((n, LANES), F32),
            jax.ShapeDtypeStruct((8, LANES), F32),
        ],
        scratch_shapes=[pltpu.VMEM((8, LANES), F32)],
        compiler_params=_cparams(("arbitrary",)),
        name="moe_route",
    )(x, g.reshape(1, d), sc, sh, wr, br)


def _expert_kernel(te_ref, nt_ref, tok_ref, h_hbm, w1_ref, b1_ref, w2_ref, b2_ref, even_ref, o_ref, xbuf, sem):
    tm = o_ref.shape[0]
    j = pl.program_id(0)
    nt = nt_ref[0]
    cur = j % 2

    def start_gather(tile, buf):
        base = tile * tm

        def body(ro, c):
            r0 = pl.multiple_of(ro * 8, 8)
            for ri in range(8):
                tok = tok_ref[base + r0 + ri]
                pltpu.make_async_copy(h_hbm.at[pl.ds(tok, 1), :], xbuf.at[buf, pl.ds(r0 + ri, 1), :],
                                      sem.at[buf]).start()
            return c
        lax.fori_loop(0, tm // 8, body, 0)

    @pl.when(j == 0)
    def _():
        start_gather(0, 0)

    @pl.when(j + 1 < nt)
    def _():
        start_gather(j + 1, 1 - cur)

    @pl.when(j < nt)
    def _():
        pltpu.make_async_copy(h_hbm.at[pl.ds(0, tm), :], xbuf.at[cur], sem.at[cur]).wait()
        x = xbuf[cur].astype(BF16)
        z = _bdot(x, w1_ref[...]) + b1_ref[...]
        z_next = pltpu.roll(z, z.shape[1] - 1, 1)
        glu = jnp.minimum(z, SWIGLU_LIMIT)
        lin = jnp.clip(z_next, -SWIGLU_LIMIT, SWIGLU_LIMIT)
        act = (glu * _sigmoid(SWIGLU_ALPHA * glu) * (lin + 1.0)).astype(BF16)
        act = _bdot(act, even_ref[...]).astype(BF16)
        o_ref[...] = _bdot(act, w2_ref[...]) + b2_ref[...]

    @pl.when(j >= nt)
    def _():
        o_ref[...] = jnp.zeros_like(o_ref)


def _experts(h2, tile_expert, n_tiles, tok, w1, b1, w2, b2):
    n, d = h2.shape
    tm = MOE_TM
    max_tiles = tile_expert.shape[0]
    ff2 = w1.shape[2]
    ff = ff2 // 2
    even = np.zeros((ff2, ff), np.float32)
    even[2 * np.arange(ff), np.arange(ff)] = 1.0
    e_map = lambda j, te, nt, tk: (te[j], 0, 0)
    return pl.pallas_call(
        _expert_kernel,
        grid_spec=pltpu.PrefetchScalarGridSpec(
            num_scalar_prefetch=3,
            grid=(max_tiles,),
            in_specs=[
                pl.BlockSpec(memory_space=pl.ANY),
                pl.BlockSpec((None, d, ff2), e_map),
                pl.BlockSpec((None, 1, ff2), e_map),
                pl.BlockSpec((None, ff, d), e_map),
                pl.BlockSpec((None, 1, d), e_map),
                pl.BlockSpec((ff2, ff), lambda j, te, nt, tk: (0, 0)),
            ],
            out_specs=pl.BlockSpec((tm, d), lambda j, te, nt, tk: (j, 0)),
            scratch_shapes=[pltpu.VMEM((2, tm, d), F32), pltpu.SemaphoreType.DMA((2,))],
        ),
        out_shape=jax.ShapeDtypeStruct((max_tiles * tm, d), F32),
        compiler_params=_cparams(("arbitrary",)),
        name="moe_experts",
    )(tile_expert, n_tiles, tok, h2, w1, b1, w2, b2, jnp.asarray(even, BF16))


def _combine_kernel(slot_ref, y_hbm, x_ref, w_ref, g_ref, gf_ref, o_ref, ybuf, sem, *, final_norm):
    tt = x_ref.shape[0]
    i = pl.program_id(0)
    nsteps = pl.num_programs(0)
    cur = i % 2

    def start_gather(step, buf):
        base = step * (tt * TOP_K)

        def body(ro, c):
            r0 = pl.multiple_of(ro * 8, 8)
            for ri in range(8):
                for k in range(TOP_K):
                    s = slot_ref[base + (r0 + ri) * TOP_K + k]
                    pltpu.make_async_copy(y_hbm.at[pl.ds(s, 1), :], ybuf.at[buf, k, pl.ds(r0 + ri, 1), :],
                                          sem.at[buf]).start()
            return c
        lax.fori_loop(0, tt // 8, body, 0)

    @pl.when(i == 0)
    def _():
        start_gather(0, 0)

    @pl.when(i + 1 < nsteps)
    def _():
        start_gather(i + 1, 1 - cur)

    for k in range(TOP_K):
        pltpu.make_async_copy(y_hbm.at[pl.ds(0, tt), :], ybuf.at[cur, k], sem.at[cur]).wait()
    w = w_ref[...]
    y = w[:, 0:1] * ybuf[cur, 0]
    for k in range(1, TOP_K):
        y = y + w[:, k:k + 1] * ybuf[cur, k]
    out = x_ref[...] + g_ref[...] * y
    if final_norm:
        r = lax.rsqrt(jnp.mean(out * out, axis=-1, keepdims=True) + NORM_EPS)
        out = (out * r) * gf_ref[...]
    o_ref[...] = out


def _combine(slots, y_slots, x, wts, gate, g_final, seq, final_norm):
    n, d = x.shape
    tt = COMB_TT
    per_b = seq // tt
    return pl.pallas_call(
        functools.partial(_combine_kernel, final_norm=final_norm),
        grid_spec=pltpu.PrefetchScalarGridSpec(
            num_scalar_prefetch=1,
            grid=(n // tt,),
            in_specs=[
                pl.BlockSpec(memory_space=pl.ANY),
                pl.BlockSpec((tt, d), lambda i, s: (i, 0)),
                pl.BlockSpec((tt, TOP_K), lambda i, s: (i, 0)),
                pl.BlockSpec((None, 1, d), lambda i, s: (i // per_b, 0, 0)),
                pl.BlockSpec((1, d), lambda i, s: (0, 0)),
            ],
            out_specs=pl.BlockSpec((tt, d), lambda i, s: (i, 0)),
            scratch_shapes=[pltpu.VMEM((2, TOP_K, tt, d), F32), pltpu.SemaphoreType.DMA((2,))],
        ),
        out_shape=jax.ShapeDtypeStruct((n, d), F32),
        compiler_params=_cparams(("arbitrary",)),
        name="moe_combine",
    )(slots, y_slots, x, wts, gate, g_final.reshape(1, d))


def _moe(x, g2, sc, sh, gate, w_router, b_router, w_exp1, b_exp1, w_exp2, b_exp2, g_final, seq, final_norm):
    n, d = x.shape
    tm = MOE_TM
    max_tiles = n * TOP_K // tm + N_EXPERTS
    h2, meta, cnt = _route(x, g2, sc, sh, w_router, b_router, seq)

    idx = meta[:, 0:TOP_K].astype(jnp.int32)
    pos = meta[:, TOP_K:2 * TOP_K].astype(jnp.int32)
    wts = meta[:, 2 * TOP_K:3 * TOP_K]
    counts = cnt[0, :N_EXPERTS].astype(jnp.int32)
    tiles_e = (counts + tm - 1) // tm
    tile_end = jnp.cumsum(tiles_e)
    offsets = (tile_end - tiles_e) * tm
    n_tiles = tile_end[-1:]
    tile_ids = jnp.arange(max_tiles, dtype=jnp.int32)
    tile_expert = jnp.minimum(jnp.sum((tile_end[None, :] <= tile_ids[:, None]).astype(jnp.int32), axis=1),
                              N_EXPERTS - 1)
    slot = offsets[idx] + pos
    tok_ids = jnp.broadcast_to(jnp.arange(n, dtype=jnp.int32)[:, None], slot.shape)
    tok = jnp.zeros((max_tiles * tm,), jnp.int32).at[slot.reshape(-1)].set(tok_ids.reshape(-1))

    y_slots = _experts(h2, tile_expert, n_tiles, tok, w_exp1.astype(BF16), b_exp1[:, None, :],
                       w_exp2.astype(BF16), b_exp2[:, None, :])
    return _combine(slot.reshape(-1), y_slots, x, wts, gate, g_final, seq, final_norm)


def _overlap_matrix(seq):
    n_cmp = (seq - CMP_BLOCK) // CMP_STRIDE + 1
    n_slc = seq // SEL_BLOCK
    cmp_start = np.arange(n_cmp) * CMP_STRIDE
    slc_start = np.arange(n_slc) * SEL_BLOCK
    ov = np.clip(np.minimum(cmp_start[:, None] + CMP_BLOCK, slc_start[None, :] + SEL_BLOCK)
                 - np.maximum(cmp_start[:, None], slc_start[None, :]), 0, None) / CMP_BLOCK
    out = np.zeros((LANES, LANES), np.float32)
    out[:n_cmp, :n_slc] = ov
    return jnp.asarray(out, BF16)


def _expand_matrix(seq):
    n_chunks = seq // ATT_TK
    key = np.arange(seq).reshape(n_chunks, 1, ATT_TK)
    blk = np.arange(LANES).reshape(1, LANES, 1)
    return jnp.asarray((key // SEL_BLOCK == blk).astype(np.float32), BF16)


def _cmp_rope_tables(seq):
    n_cmp = (seq - CMP_BLOCK) // CMP_STRIDE + 1
    half = ROPE_DIM // 2
    inv_freq = jnp.power(ROPE_THETA, -jnp.arange(half, dtype=F32) * 2.0 / ROPE_DIM)
    pos = jnp.arange(n_cmp + 1, dtype=F32) * CMP_STRIDE + (CMP_BLOCK - 1)
    ang = pos[:, None] * inv_freq[None, :]
    cos, sin = jnp.cos(ang), jnp.sin(ang)
    ones = jnp.ones((n_cmp + 1, HEAD_DIM - ROPE_DIM), F32)
    c_k = jnp.concatenate([cos, cos, ones], axis=1)
    s_k = jnp.concatenate([-sin, sin, 0.0 * ones], axis=1)
    return jnp.stack([c_k, jnp.ones_like(c_k)]), jnp.stack([s_k, jnp.zeros_like(s_k)])


def kernel(x, c, w_ada, b_ada, ada_table, g_norm1, w_in, w_cmp1, w_cmp2, cmp_pos, conv_w, conv_b,
           w_lru_gate, b_lru_gate, lru_lambda, w_branch, w_out, g_norm2, w_router, b_router,
           w_exp1, b_exp1, w_exp2, b_exp2, g_final):
    batch, seq, d = x.shape
    depth = w_in.shape[0]
    n = batch * seq
    g = N_KV_HEADS
    assert d == D_MODEL and seq % MM_TM == 0 and seq % ATT_TQ == 0 and n % COMB_TT == 0

    mod = _ada_mod(c, w_ada, b_ada, ada_table)
    cos_t, sin_t = _rope_tables(seq)
    cos_c, sin_c = _cmp_rope_tables(seq)
    ov = _overlap_matrix(seq)
    ex = _expand_matrix(seq)

    xf = x.reshape(n, d)
    for l in range(depth):
        m = [mod[l, :batch, j * d:(j + 1) * d].reshape(batch, 1, d) for j in range(N_ADA)]
        shift1, scale1, gate1, shift2, scale2, gate2 = m

        h = _norm_call(xf, g_norm1[l], scale1, shift1, seq)
        za = _proj_a(h, w_in, l, cos_t, sin_t, seq)
        zg = _proj_gate(h, w_in, l)
        zr = _proj_r(h, w_in, l)

        kv = za[:, Q_WIDTH:Q_WIDTH + 2 * KV_WIDTH].reshape(batch, seq // CMP_STRIDE, CMP_STRIDE, 2, g, HEAD_DIM)
        z2 = kv.transpose(3, 0, 4, 1, 2, 5).reshape(2, batch, g * (seq // CMP_STRIDE), CMP_STRIDE * HEAD_DIM)
        kvc = _compress(z2, w_cmp1, w_cmp2, cmp_pos, l, cos_c, sin_c)
        gl = zg[:, :GATE_W].reshape(n, g, 3 * HEADS_PER_GROUP).transpose(1, 0, 2)
        gl = jnp.pad(gl, ((0, 0), (0, 0), (0, LANES - 3 * HEADS_PER_GROUP)))
        o_attn = _attention(za, kvc, gl, ov, ex, batch, seq)

        prm = jnp.concatenate([conv_w[l], conv_b[l][None], b_lru_gate[l], lru_lambda[l][None]], axis=0)
        o_rnn = _rglru(zr, prm, w_lru_gate[l], batch, seq)

        mixed = _mix(o_attn, o_rnn, w_branch, l, zr)
        xf = _out_proj(mixed, w_out, l, xf, gate1, seq)

        xf = _moe(xf, g_norm2[l], scale2, shift2, gate2, w_router[l], b_router[l], w_exp1[l], b_exp1[l],
                  w_exp2[l], b_exp2[l], g_final, seq, final_norm=(l == depth - 1))
    return xf.reshape(batch, seq, d)
```

```python
import functools
import math

import numpy as np
import jax
import jax.numpy as jnp
from jax import lax
from jax.experimental import pallas as pl
from jax.experimental.pallas import tpu as pltpu

F32 = jnp.float32
BF16 = jnp.bfloat16

D_MODEL = 4096
HEAD_DIM = 128
N_Q_HEADS = 16
N_KV_HEADS = 4
HEADS_PER_GROUP = 4
Q_WIDTH = N_Q_HEADS * HEAD_DIM
KV_WIDTH = N_KV_HEADS * HEAD_DIM
ATTN_SCALE = HEAD_DIM ** -0.5
ROPE_THETA = 500000.0
ROPE_DIM = 32
CMP_BLOCK = 32
CMP_STRIDE = 16
CMP_HIDDEN = 256
SEL_BLOCK = 64
N_SELECT = 16
N_LOCAL_SEL = 2
WINDOW = 512
RNN_WIDTH = 2048
RNN_HEADS = 16
CONV_WIDTH = 4
LRU_C = 8.0
N_EXPERTS = 32
TOP_K = 4
EXPERT_FF = 512
SWIGLU_LIMIT = 7.0
SWIGLU_ALPHA = 1.702
N_ADA = 6
NORM_EPS = 1e-6
NEG_INF = -1e30
FORCED_SCORE = 1e6

SEG_A = Q_WIDTH + 6 * KV_WIDTH
GATE_W = 3 * N_Q_HEADS
SEG_R0 = SEG_A + GATE_W
SEG_R = 2 * RNN_WIDTH + 2 * D_MODEL

LANES = 128
VMEM_LIMIT_MB = 56
MM_TM = 1024
MM_TN = 512
ATT_TQ = 256
ATT_TK = 256
RNN_TT = 256
RNN_CB = 256
NORM_TM = 256
MOE_TM = 256
COMB_TT = 128


def _cparams(sem):
    return pltpu.CompilerParams(dimension_semantics=sem, vmem_limit_bytes=VMEM_LIMIT_MB << 20)


def _sigmoid(x):
    return 1.0 / (1.0 + jnp.exp(-x))


def _gelu_tanh(x):
    return 0.5 * x * (1.0 + jnp.tanh(0.7978845608028654 * (x + 0.044715 * (x * x * x))))


def _bdot(a, b):
    return jnp.dot(a, b, preferred_element_type=F32)


def _split_bf16(x):
    hi = x.astype(BF16)
    lo = (x - hi.astype(F32)).astype(BF16)
    return hi, lo


def _ada_kernel(c_ref, w_ref, b_ref, t_ref, o_ref):
    c = c_ref[...]
    s = (c * _sigmoid(c)).astype(BF16)
    emb = _bdot(s, w_ref[...].astype(BF16)) + b_ref[...]
    for l in range(o_ref.shape[0]):
        o_ref[l] = emb + t_ref[l]


def _ada_mod(c, w_ada, b_ada, ada_table):
    depth = ada_table.shape[0]
    b = c.shape[0]
    n = w_ada.shape[1]
    c8 = jnp.zeros((8, D_MODEL), F32).at[:b].set(c)
    tn = 512
    return pl.pallas_call(
        _ada_kernel,
        grid=(n // tn,),
        in_specs=[
            pl.BlockSpec((8, D_MODEL), lambda j: (0, 0)),
            pl.BlockSpec((D_MODEL, tn), lambda j: (0, j)),
            pl.BlockSpec((1, tn), lambda j: (0, j)),
            pl.BlockSpec((depth, 1, tn), lambda j: (0, 0, j)),
        ],
        out_specs=pl.BlockSpec((depth, 8, tn), lambda j: (0, 0, j)),
        out_shape=jax.ShapeDtypeStruct((depth, 8, n), F32),
        compiler_params=_cparams(("arbitrary",)),
        name="ada_mod",
    )(c8, w_ada, b_ada.reshape(1, n), ada_table.reshape(depth, 1, n))


def _norm_mod(x, g, sc, sh):
    r = lax.rsqrt(jnp.mean(x * x, axis=-1, keepdims=True) + NORM_EPS)
    return ((x * r) * g) * (1.0 + sc) + sh


def _norm_kernel(x_ref, g_ref, sc_ref, sh_ref, o_ref):
    o_ref[...] = _norm_mod(x_ref[...], g_ref[...], sc_ref[...], sh_ref[...]).astype(o_ref.dtype)


def _norm_call(x, g, sc, sh, seq):
    n, d = x.shape
    tm = NORM_TM
    per_b = seq // tm
    return pl.pallas_call(
        _norm_kernel,
        grid=(n // tm,),
        in_specs=[
            pl.BlockSpec((tm, d), lambda i: (i, 0)),
            pl.BlockSpec((1, d), lambda i: (0, 0)),
            pl.BlockSpec((None, 1, d), lambda i: (i // per_b, 0, 0)),
            pl.BlockSpec((None, 1, d), lambda i: (i // per_b, 0, 0)),
        ],
        out_specs=pl.BlockSpec((tm, d), lambda i: (i, 0)),
        out_shape=jax.ShapeDtypeStruct((n, d), BF16),
        compiler_params=_cparams(("arbitrary",)),
        name="norm_mod",
    )(x, g.reshape(1, d), sc, sh)


def _rope_tables(seq):
    half = ROPE_DIM // 2
    inv_freq = jnp.power(ROPE_THETA, -jnp.arange(half, dtype=F32) * 2.0 / ROPE_DIM)
    ang = jnp.arange(seq, dtype=F32)[:, None] * inv_freq[None, :]
    cos, sin = jnp.cos(ang), jnp.sin(ang)
    ones = jnp.ones((seq, HEAD_DIM - ROPE_DIM), F32)
    c_k = jnp.concatenate([cos, cos, ones], axis=1)
    s_k = jnp.concatenate([-sin, sin, 0.0 * ones], axis=1)
    c_id = jnp.ones((seq, HEAD_DIM), F32)
    s_id = jnp.zeros((seq, HEAD_DIM), F32)
    return (jnp.stack([c_k * ATTN_SCALE, c_k, c_id]), jnp.stack([s_k * ATTN_SCALE, s_k, s_id]))


def _rope_apply(x, cos, sin):
    heads = x.shape[1] // HEAD_DIM
    lane = lax.broadcasted_iota(jnp.int32, x.shape, 1) & (HEAD_DIM - 1)
    half = ROPE_DIM // 2
    fwd = pltpu.roll(x, half, 1)
    bwd = pltpu.roll(x, x.shape[1] - half, 1)
    sw = jnp.where(lane < half, bwd, fwd)
    if heads > 1:
        cos = jnp.concatenate([cos] * heads, axis=1)
        sin = jnp.concatenate([sin] * heads, axis=1)
    return x * cos + sw * sin


def _dot_nt(a, wt):
    return lax.dot_general(a, wt.astype(BF16), (((1,), (1,)), ((), ())), preferred_element_type=F32)


def _proj_rope_kernel(a_ref, w_ref, cos_ref, sin_ref, o_ref):
    acc = _dot_nt(a_ref[...], w_ref[...])
    o_ref[...] = _rope_apply(acc, cos_ref[...], sin_ref[...]).astype(o_ref.dtype)


def _proj_kernel(a_ref, w_ref, o_ref):
    w = w_ref[0] if len(w_ref.shape) == 3 else w_ref[...]
    o_ref[...] = _dot_nt(a_ref[...], w).astype(o_ref.dtype)


def _proj_a(h, w_in, layer, cos_t, sin_t, seq):
    n, d = h.shape
    tm, tn = MM_TM, MM_TN
    per_b = seq // tm

    def variant(j):
        return jnp.where(j < 4, 0, jnp.where((j == 6) | (j == 8), 1, 2))

    return pl.pallas_call(
        _proj_rope_kernel,
        grid=(n // tm, SEG_A // tn),
        in_specs=[
            pl.BlockSpec((tm, d), lambda i, j: (i, 0)),
            pl.BlockSpec((None, tn, d), lambda i, j: (layer, j, 0)),
            pl.BlockSpec((None, tm, HEAD_DIM), lambda i, j: (variant(j), i % per_b, 0)),
            pl.BlockSpec((None, tm, HEAD_DIM), lambda i, j: (variant(j), i % per_b, 0)),
        ],
        out_specs=pl.BlockSpec((tm, tn), lambda i, j: (i, j)),
        out_shape=jax.ShapeDtypeStruct((n, SEG_A), BF16),
        compiler_params=_cparams(("arbitrary", "arbitrary")),
        name="proj_qkv",
    )(h, w_in, cos_t, sin_t)


def _proj_gate(h, w_in, layer):
    n, d = h.shape
    tm = MM_TM
    return pl.pallas_call(
        _proj_kernel,
        grid=(n // tm,),
        in_specs=[
            pl.BlockSpec((tm, d), lambda i: (i, 0)),
            pl.BlockSpec((None, LANES, d), lambda i: (layer, SEG_A // LANES, 0)),
        ],
        out_specs=pl.BlockSpec((tm, LANES), lambda i: (i, 0)),
        out_shape=jax.ShapeDtypeStruct((n, LANES), F32),
        compiler_params=_cparams(("arbitrary",)),
        name="proj_gate",
    )(h, w_in)


def _proj_r(h, w_in, layer):
    n, d = h.shape
    tm, tn = MM_TM, MM_TN
    return pl.pallas_call(
        _proj_kernel,
        grid=(n // tm, SEG_R // tn),
        in_specs=[
            pl.BlockSpec((tm, d), lambda i, j: (i, 0)),
            pl.BlockSpec((pl.Element(1), pl.Element(tn), pl.Element(d)), lambda i, j: (layer, pl.multiple_of(SEG_R0 + j * tn, 8), 0)),
        ],
        out_specs=pl.BlockSpec((tm, tn), lambda i, j: (i, j)),
        out_shape=jax.ShapeDtypeStruct((n, SEG_R), F32),
        compiler_params=_cparams(("arbitrary", "arbitrary")),
        name="proj_rnn_mix",
    )(h, w_in)


def _cmp_kernel(z_ref, w1_ref, w2_ref, pos_ref, cos_ref, sin_ref, o_ref):
    z = z_ref[...]
    w1 = w1_ref[...].astype(BF16)
    half = w1.shape[0] // 2
    a = _bdot(z, w1[:half])
    b = _bdot(z, w1[half:])
    pos = jnp.broadcast_to(pos_ref[...], (8, pos_ref.shape[1])).astype(BF16)
    bias = _bdot(pos, w1)[0:1]
    y = a + pltpu.roll(b, b.shape[0] - 1, 0) + bias
    hid = _gelu_tanh(y).astype(BF16)
    out = _bdot(hid, w2_ref[...].astype(BF16))
    groups = out.shape[0] // cos_ref.shape[0]
    cos = jnp.concatenate([cos_ref[...]] * groups, axis=0)
    sin = jnp.concatenate([sin_ref[...]] * groups, axis=0)
    o_ref[...] = _rope_apply(out, cos, sin).astype(o_ref.dtype)


def _compress(z2, w_cmp1, w_cmp2, cmp_pos, layer, cos_c, sin_c):
    _, b, rows, width = z2.shape
    n_grp = cos_c.shape[1]
    return pl.pallas_call(
        _cmp_kernel,
        grid=(2, b),
        in_specs=[
            pl.BlockSpec((None, None, rows, width), lambda j, bi: (j, bi, 0, 0)),
            pl.BlockSpec((None, None, 2 * width, CMP_HIDDEN), lambda j, bi: (layer, j, 0, 0)),
            pl.BlockSpec((None, None, CMP_HIDDEN, HEAD_DIM), lambda j, bi: (layer, j, 0, 0)),
            pl.BlockSpec((None, None, 1, 2 * width), lambda j, bi: (layer, j, 0, 0)),
            pl.BlockSpec((None, n_grp, HEAD_DIM), lambda j, bi: (j, 0, 0)),
            pl.BlockSpec((None, n_grp, HEAD_DIM), lambda j, bi: (j, 0, 0)),
        ],
        out_specs=pl.BlockSpec((None, None, rows, HEAD_DIM), lambda j, bi: (j, bi, 0, 0)),
        out_shape=jax.ShapeDtypeStruct((2, b, rows, HEAD_DIM), BF16),
        compiler_params=_cparams(("arbitrary", "arbitrary")),
        name="nsa_compress",
    )(z2, w_cmp1, w_cmp2, cmp_pos.reshape(cmp_pos.shape[0], 2, 1, CMP_BLOCK * HEAD_DIM), cos_c, sin_c)


def _attn_t_kernel(q_ref, kc_ref, vc_ref, ks_ref, vst_ref, kw_ref, vwt_ref, gl_ref, ovt_ref, ext_ref, o_ref):
    tq, tk = ATT_TQ, ATT_TK
    hpg = HEADS_PER_GROUP
    rows = hpg * tq
    i = pl.program_id(2)
    qb = q_ref[...]
    q4 = jnp.concatenate([qb[:, h * HEAD_DIM:(h + 1) * HEAD_DIM] for h in range(hpg)], axis=0)
    dn_nt = (((1,), (1,)), ((), ()))

    def tile_heads(x):
        return jnp.concatenate([x] * hpg, axis=1)

    n_cmp = kc_ref.shape[0] - 1
    key_n = lax.broadcasted_iota(jnp.int32, (LANES, tq), 0)
    t_pos = i * tq + lax.broadcasted_iota(jnp.int32, (LANES, tq), 1)
    cmask = (key_n < n_cmp) & (key_n * CMP_STRIDE + (CMP_BLOCK - 1) <= t_pos)
    cb4 = tile_heads(jnp.where(cmask, 0.0, NEG_INF))
    s = lax.dot_general(kc_ref[...], q4, dn_nt, preferred_element_type=F32) + cb4
    m = jnp.max(s, axis=0, keepdims=True)
    e = jnp.where(cb4 > -1.0, jnp.exp(s - m), 0.0)
    den = jnp.sum(e, axis=0, keepdims=True)
    p = e / jnp.maximum(den, 1e-30)
    vc_t = jnp.transpose(vc_ref[...].astype(F32)).astype(BF16)
    o_cmp = _bdot(vc_t, p.astype(BF16))
    p_sum = p[:, 0:tq]
    for h in range(1, hpg):
        p_sum = p_sum + p[:, h * tq:(h + 1) * tq]
    p_hi, p_lo = _split_bf16(p_sum)
    n_slc = ext_ref.shape[0] * (tk // SEL_BLOCK)
    imp = (_bdot(ovt_ref[...], p_hi) + _bdot(ovt_ref[...], p_lo))[:n_slc]

    blk = lax.broadcasted_iota(jnp.int32, (n_slc, tq), 0)
    cur = lax.shift_right_logical(i * tq + lax.broadcasted_iota(jnp.int32, (n_slc, tq), 1), int(math.log2(SEL_BLOCK)))
    visible = blk <= cur
    forced = (blk == 0) | (visible & (blk > cur - N_LOCAL_SEL))
    score = jnp.where(visible, jnp.where(forced, FORCED_SCORE, imp), NEG_INF)
    rank = jnp.zeros((n_slc, tq), F32)
    for k in range(n_slc):
        row = score[k:k + 1, :]
        beats = (row > score) | ((row == score) & (blk > k))
        rank = rank + jnp.where(beats, 1.0, 0.0)
    sel = jnp.where((rank < float(min(N_SELECT, n_slc))) & visible, 1.0, 0.0)
    sel = jnp.concatenate([sel, jnp.zeros((LANES - n_slc, tq), F32)], axis=0).astype(BF16)

    rel = (lax.broadcasted_iota(jnp.int32, (tk, tq), 1) + i * tq) - lax.broadcasted_iota(jnp.int32, (tk, tq), 0)

    def flash_step(c, carry, k_ref, vt_ref, bias):
        m_p, l_p, acc_p = carry
        off = pl.multiple_of(c * tk, tk)
        kk = k_ref[pl.ds(off, tk), :]
        sc = lax.dot_general(kk, q4, dn_nt, preferred_element_type=F32) + tile_heads(bias)
        m_n = jnp.maximum(m_p, jnp.max(sc, axis=0, keepdims=True))
        alpha = jnp.exp(m_p - m_n)
        pp = jnp.exp(sc - m_n)
        l_n = alpha * l_p + jnp.sum(pp, axis=0, keepdims=True)
        acc_n = alpha * acc_p + _bdot(vt_ref[c], pp.astype(BF16))
        return m_n, l_n, acc_n

    init = (jnp.full((1, rows), NEG_INF, F32), jnp.zeros((1, rows), F32), jnp.zeros((HEAD_DIM, rows), F32))

    def sel_body(c, carry):
        kmask = _bdot(ext_ref[c], sel) > 0.5
        ok = kmask & (rel - c * tk >= 0)
        return flash_step(c, carry, ks_ref, vst_ref, jnp.where(ok, 0.0, NEG_INF))

    _, l_s, acc_s = lax.fori_loop(0, i + 1, sel_body, init)

    def win_body(c, carry):
        d = rel - c * tk
        ok = (d >= 0) & (d < WINDOW)
        return flash_step(c, carry, kw_ref, vwt_ref, jnp.where(ok, 0.0, NEG_INF))

    c0 = jnp.maximum(i - WINDOW // tk, 0)
    _, l_w, acc_w = lax.fori_loop(c0, i + 1, win_body, init)

    o_sel = acc_s / l_s
    o_win = acc_w / l_w
    gates = _sigmoid(gl_ref[...])
    outs = []
    for h in range(hpg):
        sl = slice(h * tq, (h + 1) * tq)
        o_h = (gates[3 * h:3 * h + 1] * o_cmp[:, sl] + gates[3 * h + 1:3 * h + 2] * o_sel[:, sl]
               + gates[3 * h + 2:3 * h + 3] * o_win[:, sl])
        outs.append(jnp.transpose(o_h))
    o_ref[...] = jnp.concatenate(outs, axis=1).astype(o_ref.dtype)


def _attention_t(za, kvc, vst, vwt, gl_t, ov_t, ex_t, batch, seq):
    n = za.shape[0]
    tq = ATT_TQ
    per_b = seq // tq
    g = N_KV_HEADS
    n_cb = kvc.shape[2] // g
    kv_blk = lambda col0: pl.BlockSpec((seq, HEAD_DIM), lambda b, gi, i: (b, col0 + gi))
    vt_blk = pl.BlockSpec((None, None) + vst.shape[2:], lambda b, gi, i: (b, gi, 0, 0, 0))
    q_cols = Q_WIDTH // HEAD_DIM
    return pl.pallas_call(
        _attn_t_kernel,
        grid=(batch, g, per_b),
        in_specs=[
            pl.BlockSpec((tq, HEADS_PER_GROUP * HEAD_DIM), lambda b, gi, i: (b * per_b + i, gi)),
            pl.BlockSpec((None, None, n_cb, HEAD_DIM), lambda b, gi, i: (0, b, gi, 0)),
            pl.BlockSpec((None, None, n_cb, HEAD_DIM), lambda b, gi, i: (1, b, gi, 0)),
            kv_blk(q_cols + 2 * g), vt_blk, kv_blk(q_cols + 4 * g), vt_blk,
            pl.BlockSpec((None, gl_t.shape[1], tq), lambda b, gi, i: (gi, 0, b * per_b + i)),
            pl.BlockSpec((LANES, LANES), lambda b, gi, i: (0, 0)),
            pl.BlockSpec(ex_t.shape, lambda b, gi, i: (0, 0, 0)),
        ],
        out_specs=pl.BlockSpec((tq, HEADS_PER_GROUP * HEAD_DIM), lambda b, gi, i: (b * per_b + i, gi)),
        out_shape=jax.ShapeDtypeStruct((n, Q_WIDTH), BF16),
        compiler_params=_cparams(("arbitrary", "arbitrary", "arbitrary")),
        name="nsa_attention",
    )(za, kvc, kvc, za, vst, za, vwt, gl_t, ov_t, ex_t)


def _rglru_kernel(x_ref, g_ref, p_ref, wg_ref, o_ref, xbuf, hc):
    tt, cb = x_ref.shape
    step = pl.program_id(2)

    @pl.when(step == 0)
    def _():
        xbuf[0:8, :] = jnp.zeros((8, cb), F32)
        hc[...] = jnp.zeros_like(hc)

    x = x_ref[...]
    xbuf[8:8 + tt, :] = x
    prm = p_ref[...]
    xc = prm[4:5] + prm[3:4] * x
    for k in range(CONV_WIDTH - 1):
        shift = CONV_WIDTH - 1 - k
        xc = xc + prm[k:k + 1] * xbuf[8 - shift:8 - shift + tt, :]
    xbuf[0:8, :] = x[tt - 8:tt]

    xcb = xc.astype(BF16)
    pre_r, pre_i = [], []
    for h in range(cb // HEAD_DIM):
        xh = xcb[:, h * HEAD_DIM:(h + 1) * HEAD_DIM]
        pre_r.append(_bdot(xh, wg_ref[0, h].astype(BF16)))
        pre_i.append(_bdot(xh, wg_ref[1, h].astype(BF16)))
    r = _sigmoid(jnp.concatenate(pre_r, axis=1) + prm[5:6])
    ig = _sigmoid(jnp.concatenate(pre_i, axis=1) + prm[6:7])
    zl = -prm[7:8]
    softplus = jnp.maximum(zl, 0.0) + jnp.log1p(jnp.exp(-jnp.abs(zl)))
    log_a = (-LRU_C) * r * softplus
    a = jnp.exp(log_a)
    b = jnp.sqrt(1.0 - a * a) * (ig * xc)

    row = lax.broadcasted_iota(jnp.int32, (tt, cb), 0)
    s = 1
    while s < tt:
        keep = row >= s
        a_sh = jnp.where(keep, pltpu.roll(a, s, 0), 1.0)
        b_sh = jnp.where(keep, pltpu.roll(b, s, 0), 0.0)
        b = a * b_sh + b
        a = a * a_sh
        s *= 2
    hcur = b + a * hc[0:1, :]
    hc[...] = jnp.broadcast_to(hcur[tt - 1:tt], hc.shape)
    o_ref[...] = (hcur * _gelu_tanh(g_ref[...])).astype(o_ref.dtype)


def _rglru(zr, prm, w_gate, batch, seq):
    n = zr.shape[0]
    tt, cb = RNN_TT, RNN_CB
    per_b = seq // tt
    n_cb = RNN_WIDTH // cb
    hpb = cb // HEAD_DIM
    return pl.pallas_call(
        _rglru_kernel,
        grid=(batch, n_cb, per_b),
        in_specs=[
            pl.BlockSpec((tt, cb), lambda b, c, t: (b * per_b + t, c)),
            pl.BlockSpec((tt, cb), lambda b, c, t: (b * per_b + t, n_cb + c)),
            pl.BlockSpec((8, cb), lambda b, c, t: (0, c)),
            pl.BlockSpec((2, hpb, HEAD_DIM, HEAD_DIM), lambda b, c, t: (0, c, 0, 0)),
        ],
        out_specs=pl.BlockSpec((tt, cb), lambda b, c, t: (b * per_b + t, c)),
        out_shape=jax.ShapeDtypeStruct((n, RNN_WIDTH), BF16),
        scratch_shapes=[pltpu.VMEM((tt + 8, cb), F32), pltpu.VMEM((8, cb), F32)],
        compiler_params=_cparams(("arbitrary", "arbitrary", "arbitrary")),
        name="rglru",
    )(zr, zr, prm, w_gate)


def _mix_kernel(a1_ref, a2_ref, w1_ref, w2_ref, m1_ref, m2_ref, o_ref):
    y1 = _bdot(a1_ref[...], w1_ref[...].astype(BF16))
    y2 = _bdot(a2_ref[...], w2_ref[...].astype(BF16))
    o_ref[...] = (_sigmoid(m1_ref[...]) * y1 + _sigmoid(m2_ref[...]) * y2).astype(o_ref.dtype)


def _mix(o_attn, o_rnn, w_branch, layer, zr):
    n, k = o_attn.shape
    tm, tn = MM_TM, 256
    m_off = 2 * RNN_WIDTH // tn
    return pl.pallas_call(
        _mix_kernel,
        grid=(n // tm, D_MODEL // tn),
        in_specs=[
            pl.BlockSpec((tm, k), lambda i, j: (i, 0)),
            pl.BlockSpec((tm, k), lambda i, j: (i, 0)),
            pl.BlockSpec((None, None, k, tn), lambda i, j: (layer, 0, 0, j)),
            pl.BlockSpec((None, None, k, tn), lambda i, j: (layer, 1, 0, j)),
            pl.BlockSpec((tm, tn), lambda i, j: (i, m_off + j)),
            pl.BlockSpec((tm, tn), lambda i, j: (i, m_off + D_MODEL // tn + j)),
        ],
        out_specs=pl.BlockSpec((tm, tn), lambda i, j: (i, j)),
        out_shape=jax.ShapeDtypeStruct((n, D_MODEL), BF16),
        compiler_params=_cparams(("arbitrary", "arbitrary")),
        name="mixer_merge",
    )(o_attn, o_rnn, w_branch, w_branch, zr, zr)


def _out_kernel(a_ref, w_ref, x_ref, g_ref, o_ref):
    o_ref[...] = x_ref[...] + g_ref[...] * _bdot(a_ref[...], w_ref[...].astype(BF16))


def _out_proj(mixed, w_out, layer, x, gate, seq):
    n, d = x.shape
    tm, tn = MM_TM, 256
    per_b = seq // tm
    return pl.pallas_call(
        _out_kernel,
        grid=(n // tm, d // tn),
        in_specs=[
            pl.BlockSpec((tm, d), lambda i, j: (i, 0)),
            pl.BlockSpec((None, d, tn), lambda i, j: (layer, 0, j)),
            pl.BlockSpec((tm, tn), lambda i, j: (i, j)),
            pl.BlockSpec((None, 1, tn), lambda i, j: (i // per_b, 0, j)),
        ],
        out_specs=pl.BlockSpec((tm, tn), lambda i, j: (i, j)),
        out_shape=jax.ShapeDtypeStruct((n, d), F32),
        compiler_params=_cparams(("arbitrary", "arbitrary")),
        name="out_proj",
    )(mixed, w_out, x, gate)


def _route_kernel(x_ref, g_ref, sc_ref, sh_ref, wr_ref, br_ref, h_ref, meta_ref, cnt_ref, run):
    tm = x_ref.shape[0]

    @pl.when(pl.program_id(0) == 0)
    def _():
        run[...] = jnp.zeros_like(run)

    h = _norm_mod(x_ref[...], g_ref[...], sc_ref[...], sh_ref[...])
    h_ref[...] = h
    h_hi, h_lo = _split_bf16(h)
    w_hi, w_lo = _split_bf16(wr_ref[...])
    logits = _bdot(h_hi, w_hi) + _bdot(h_hi, w_lo) + _bdot(h_lo, w_hi) + br_ref[...]
    lane = lax.broadcasted_iota(jnp.int32, (tm, LANES), 1)
    lane_f = lane.astype(F32)
    cur = jnp.where(lane < N_EXPERTS, logits, -jnp.inf)
    vals, idxs = [], []
    for _ in range(TOP_K):
        mx = jnp.max(cur, axis=-1, keepdims=True)
        ix = jnp.min(jnp.where(cur == mx, lane_f, float(LANES)), axis=-1, keepdims=True)
        vals.append(mx)
        idxs.append(ix)
        cur = jnp.where(lane_f == ix, -jnp.inf, cur)
    exps = [jnp.exp(v - vals[0]) for v in vals]
    den = exps[0]
    for e in exps[1:]:
        den = den + e
    hot = [lane_f == ix for ix in idxs]
    cnt = jnp.zeros((tm, LANES), F32)
    for hk in hot:
        cnt = cnt + jnp.where(hk, 1.0, 0.0)
    tri = jnp.where(lax.broadcasted_iota(jnp.int32, (tm, tm), 0) > lax.broadcasted_iota(jnp.int32, (tm, tm), 1),
                    1.0, 0.0).astype(BF16)
    pos = _bdot(tri, cnt.astype(BF16)) + run[0:1, :]
    meta = jnp.zeros((tm, LANES), F32)
    for k in range(TOP_K):
        pk = jnp.sum(jnp.where(hot[k], pos, 0.0), axis=-1, keepdims=True)
        meta = jnp.where(lane == k, idxs[k], meta)
        meta = jnp.where(lane == TOP_K + k, pk, meta)
        meta = jnp.where(lane == 2 * TOP_K + k, exps[k] / den, meta)
    meta_ref[...] = meta
    run[...] = run[...] + jnp.sum(cnt, axis=0, keepdims=True)
    cnt_ref[...] = run[...]


def _route(x, g, sc, sh, w_router, b_router, seq):
    n, d = x.shape
    tm = NORM_TM
    per_b = seq // tm
    wr = jnp.zeros((d, LANES), F32).at[:, :N_EXPERTS].set(w_router)
    br = jnp.zeros((1, LANES), F32).at[0, :N_EXPERTS].set(b_router)
    return pl.pallas_call(
        _route_kernel,
        grid=(n // tm,),
        in_specs=[
            pl.BlockSpec((tm, d), lambda i: (i, 0)),
            pl.BlockSpec((1, d), lambda i: (0, 0)),
            pl.BlockSpec((None, 1, d), lambda i: (i // per_b, 0, 0)),
            pl.BlockSpec((None, 1, d), lambda i: (i // per_b, 0, 0)),
            pl.BlockSpec((d, LANES), lambda i: (0, 0)),
            pl.BlockSpec((1, LANES), lambda i: (0, 0)),
        ],
        out_specs=[
            pl.BlockSpec((tm, d), lambda i: (i, 0)),
            pl.BlockSpec((tm, LANES), lambda i: (i, 0)),
            pl.BlockSpec((8, LANES), lambda i: (0, 0)),
        ],
        out_shape=[
            jax.ShapeDtypeStruct((n, d), F32),
            jax.ShapeDtypeStruct((n, LANES), F32),
            jax.ShapeDtypeStruct((8, LANES), F32),
        ],
        scratch_shapes=[pltpu.VMEM((8, LANES), F32)],
        compiler_params=_cparams(("arbitrary",)),
        name="moe_route",
    )(x, g.reshape(1, d), sc, sh, wr, br)


def _expert_kernel(te_ref, nt_ref, tok_ref, h_hbm, w1_ref, b1_ref, w2_ref, b2_ref, even_ref, o_ref, xbuf, sem):
    tm = o_ref.shape[0]
    j = pl.program_id(0)
    nt = nt_ref[0]
    cur = j % 2

    def start_gather(tile, buf):
        base = tile * tm

        def body(ro, c):
            r0 = pl.multiple_of(ro * 8, 8)
            for ri in range(8):
                tok = tok_ref[base + r0 + ri]
                pltpu.make_async_copy(h_hbm.at[pl.ds(tok, 1), :], xbuf.at[buf, pl.ds(r0 + ri, 1), :],
                                      sem.at[buf]).start()
            return c
        lax.fori_loop(0, tm // 8, body, 0)

    @pl.when(j == 0)
    def _():
        start_gather(0, 0)

    @pl.when(j + 1 < nt)
    def _():
        start_gather(j + 1, 1 - cur)

    @pl.when(j < nt)
    def _():
        pltpu.make_async_copy(h_hbm.at[pl.ds(0, tm), :], xbuf.at[cur], sem.at[cur]).wait()
        x = xbuf[cur].astype(BF16)
        z = _bdot(x, w1_ref[...]) + b1_ref[...]
        z_next = pltpu.roll(z, z.shape[1] - 1, 1)
        glu = jnp.minimum(z, SWIGLU_LIMIT)
        lin = jnp.clip(z_next, -SWIGLU_LIMIT, SWIGLU_LIMIT)
        act = (glu * _sigmoid(SWIGLU_ALPHA * glu) * (lin + 1.0)).astype(BF16)
        act = _bdot(act, even_ref[...]).astype(BF16)
        o_ref[...] = _bdot(act, w2_ref[...]) + b2_ref[...]

    @pl.when(j >= nt)
    def _():
        o_ref[...] = jnp.zeros_like(o_ref)


def _experts(h2, tile_expert, n_tiles, tok, w1, b1, w2, b2, layer):
    n, d = h2.shape
    tm = MOE_TM
    max_tiles = tile_expert.shape[0]
    ff2 = w1.shape[3]
    ff = ff2 // 2
    even = np.zeros((ff2, ff), np.float32)
    even[2 * np.arange(ff), np.arange(ff)] = 1.0
    e_map = lambda j, te, nt, tk: (layer, te[j], 0, 0)
    return pl.pallas_call(
        _expert_kernel,
        grid_spec=pltpu.PrefetchScalarGridSpec(
            num_scalar_prefetch=3,
            grid=(max_tiles,),
            in_specs=[
                pl.BlockSpec(memory_space=pl.ANY),
                pl.BlockSpec((None, None, d, ff2), e_map),
                pl.BlockSpec((None, None, 1, ff2), e_map),
                pl.BlockSpec((None, None, ff, d), e_map),
                pl.BlockSpec((None, None, 1, d), e_map),
                pl.BlockSpec((ff2, ff), lambda j, te, nt, tk: (0, 0)),
            ],
            out_specs=pl.BlockSpec((tm, d), lambda j, te, nt, tk: (j, 0)),
            scratch_shapes=[pltpu.VMEM((2, tm, d), F32), pltpu.SemaphoreType.DMA((2,))],
        ),
        out_shape=jax.ShapeDtypeStruct((max_tiles * tm, d), F32),
        compiler_params=_cparams(("arbitrary",)),
        name="moe_experts",
    )(tile_expert, n_tiles, tok, h2, w1, b1, w2, b2, jnp.asarray(even, BF16))


def _combine_kernel(slot_ref, y_hbm, x_ref, w_ref, g_ref, gf_ref, o_ref, ybuf, sem, *, final_norm):
    tt = x_ref.shape[0]
    i = pl.program_id(0)
    nsteps = pl.num_programs(0)
    cur = i % 2

    def start_gather(step, buf):
        base = step * (tt * TOP_K)

        def body(ro, c):
            r0 = pl.multiple_of(ro * 8, 8)
            for ri in range(8):
                for k in range(TOP_K):
                    s = slot_ref[base + (r0 + ri) * TOP_K + k]
                    pltpu.make_async_copy(y_hbm.at[pl.ds(s, 1), :], ybuf.at[buf, k, pl.ds(r0 + ri, 1), :],
                                          sem.at[buf]).start()
            return c
        lax.fori_loop(0, tt // 8, body, 0)

    @pl.when(i == 0)
    def _():
        start_gather(0, 0)

    @pl.when(i + 1 < nsteps)
    def _():
        start_gather(i + 1, 1 - cur)

    for k in range(TOP_K):
        pltpu.make_async_copy(y_hbm.at[pl.ds(0, tt), :], ybuf.at[cur, k], sem.at[cur]).wait()
    w = w_ref[...]
    y = w[:, 0:1] * ybuf[cur, 0]
    for k in range(1, TOP_K):
        y = y + w[:, k:k + 1] * ybuf[cur, k]
    out = x_ref[...] + g_ref[...] * y
    if final_norm:
        r = lax.rsqrt(jnp.mean(out * out, axis=-1, keepdims=True) + NORM_EPS)
        out = (out * r) * gf_ref[...]
    o_ref[...] = out


def _combine(slots, y_slots, x, wts, gate, g_final, seq, final_norm):
    n, d = x.shape
    tt = COMB_TT
    per_b = seq // tt
    return pl.pallas_call(
        functools.partial(_combine_kernel, final_norm=final_norm),
        grid_spec=pltpu.PrefetchScalarGridSpec(
            num_scalar_prefetch=1,
            grid=(n // tt,),
            in_specs=[
                pl.BlockSpec(memory_space=pl.ANY),
                pl.BlockSpec((tt, d), lambda i, s: (i, 0)),
                pl.BlockSpec((tt, TOP_K), lambda i, s: (i, 0)),
                pl.BlockSpec((None, 1, d), lambda i, s: (i // per_b, 0, 0)),
                pl.BlockSpec((1, d), lambda i, s: (0, 0)),
            ],
            out_specs=pl.BlockSpec((tt, d), lambda i, s: (i, 0)),
            scratch_shapes=[pltpu.VMEM((2, TOP_K, tt, d), F32), pltpu.SemaphoreType.DMA((2,))],
        ),
        out_shape=jax.ShapeDtypeStruct((n, d), F32),
        compiler_params=_cparams(("arbitrary",)),
        name="moe_combine",
    )(slots, y_slots, x, wts, gate, g_final.reshape(1, d))


def _moe(x, g2, sc, sh, gate, w_router, b_router, w1, b1, w2, b2, layer, g_final, seq, final_norm):
    n, d = x.shape
    tm = MOE_TM
    max_tiles = n * TOP_K // tm + N_EXPERTS
    h2, meta, cnt = _route(x, g2, sc, sh, w_router, b_router, seq)

    idx = meta[:, 0:TOP_K].astype(jnp.int32)
    pos = meta[:, TOP_K:2 * TOP_K].astype(jnp.int32)
    wts = meta[:, 2 * TOP_K:3 * TOP_K]
    counts = cnt[0, :N_EXPERTS].astype(jnp.int32)
    tiles_e = (counts + tm - 1) // tm
    tile_end = jnp.cumsum(tiles_e)
    offsets = (tile_end - tiles_e) * tm
    n_tiles = tile_end[-1:]
    tile_ids = jnp.arange(max_tiles, dtype=jnp.int32)
    tile_expert = jnp.minimum(jnp.sum((tile_end[None, :] <= tile_ids[:, None]).astype(jnp.int32), axis=1),
                              N_EXPERTS - 1)
    slot = offsets[idx] + pos
    tok_ids = jnp.broadcast_to(jnp.arange(n, dtype=jnp.int32)[:, None], slot.shape)
    tok = jnp.zeros((max_tiles * tm,), jnp.int32).at[slot.reshape(-1)].set(tok_ids.reshape(-1))

    y_slots = _experts(h2, tile_expert, n_tiles, tok, w1, b1, w2, b2, layer)
    return _combine(slot.reshape(-1), y_slots, x, wts, gate, g_final, seq, final_norm)


def _overlap_matrix(seq):
    n_cmp = (seq - CMP_BLOCK) // CMP_STRIDE + 1
    n_slc = seq // SEL_BLOCK
    cmp_start = np.arange(n_cmp) * CMP_STRIDE
    slc_start = np.arange(n_slc) * SEL_BLOCK
    ov = np.clip(np.minimum(cmp_start[:, None] + CMP_BLOCK, slc_start[None, :] + SEL_BLOCK)
                 - np.maximum(cmp_start[:, None], slc_start[None, :]), 0, None) / CMP_BLOCK
    out = np.zeros((LANES, LANES), np.float32)
    out[:n_slc, :n_cmp] = ov.T
    return jnp.asarray(out, BF16)


def _expand_matrix(seq):
    n_chunks = seq // ATT_TK
    key = np.arange(seq).reshape(n_chunks, ATT_TK, 1)
    blk = np.arange(LANES).reshape(1, 1, LANES)
    return jnp.asarray((key // SEL_BLOCK == blk).astype(np.float32), BF16)


def _cmp_rope_tables(seq):
    n_cmp = (seq - CMP_BLOCK) // CMP_STRIDE + 1
    half = ROPE_DIM // 2
    inv_freq = jnp.power(ROPE_THETA, -jnp.arange(half, dtype=F32) * 2.0 / ROPE_DIM)
    pos = jnp.arange(n_cmp + 1, dtype=F32) * CMP_STRIDE + (CMP_BLOCK - 1)
    ang = pos[:, None] * inv_freq[None, :]
    cos, sin = jnp.cos(ang), jnp.sin(ang)
    ones = jnp.ones((n_cmp + 1, HEAD_DIM - ROPE_DIM), F32)
    c_k = jnp.concatenate([cos, cos, ones], axis=1)
    s_k = jnp.concatenate([-sin, sin, 0.0 * ones], axis=1)
    return jnp.stack([c_k, jnp.ones_like(c_k)]), jnp.stack([s_k, jnp.zeros_like(s_k)])


def kernel(x, c, w_ada, b_ada, ada_table, g_norm1, w_in, w_cmp1, w_cmp2, cmp_pos, conv_w, conv_b,
           w_lru_gate, b_lru_gate, lru_lambda, w_branch, w_out, g_norm2, w_router, b_router,
           w_exp1, b_exp1, w_exp2, b_exp2, g_final):
    batch, seq, d = x.shape
    depth = w_in.shape[0]
    n = batch * seq
    g = N_KV_HEADS
    assert d == D_MODEL and seq % MM_TM == 0 and seq % ATT_TQ == 0 and n % COMB_TT == 0

    mod = _ada_mod(c, w_ada, b_ada, ada_table)
    cos_t, sin_t = _rope_tables(seq)
    cos_c, sin_c = _cmp_rope_tables(seq)
    ov_t = _overlap_matrix(seq)
    ex_t = _expand_matrix(seq)
    w_in = jnp.swapaxes(w_in, 1, 2)
    w1_bf = w_exp1.astype(BF16)
    w2_bf = w_exp2.astype(BF16)
    b1_e = b_exp1[:, :, None, :]
    b2_e = b_exp2[:, :, None, :]

    xf = x.reshape(n, d)
    for l in range(depth):
        m = [mod[l, :batch, j * d:(j + 1) * d].reshape(batch, 1, d) for j in range(N_ADA)]
        shift1, scale1, gate1, shift2, scale2, gate2 = m

        h = _norm_call(xf, g_norm1[l], scale1, shift1, seq)
        za = _proj_a(h, w_in, l, cos_t, sin_t, seq)
        zg = _proj_gate(h, w_in, l)
        zr = _proj_r(h, w_in, l)

        kv = za[:, Q_WIDTH:Q_WIDTH + 2 * KV_WIDTH].reshape(batch, seq // CMP_STRIDE, CMP_STRIDE, 2, g, HEAD_DIM)
        z2 = kv.transpose(3, 0, 4, 1, 2, 5).reshape(2, batch, g * (seq // CMP_STRIDE), CMP_STRIDE * HEAD_DIM)
        kvc = _compress(z2, w_cmp1, w_cmp2, cmp_pos, l, cos_c, sin_c)
        gl_t = zg[:, :GATE_W].reshape(n, g, 3 * HEADS_PER_GROUP).transpose(1, 2, 0)
        gl_t = jnp.pad(gl_t, ((0, 0), (0, 16 - 3 * HEADS_PER_GROUP), (0, 0)))

        def values_t(col0):
            v = za[:, col0:col0 + KV_WIDTH].reshape(batch, seq // ATT_TK, ATT_TK, g, HEAD_DIM)
            return v.transpose(0, 3, 1, 4, 2)

        vst = values_t(Q_WIDTH + 3 * KV_WIDTH)
        vwt = values_t(Q_WIDTH + 5 * KV_WIDTH)
        o_attn = _attention_t(za, kvc, vst, vwt, gl_t, ov_t, ex_t, batch, seq)

        prm = jnp.concatenate([conv_w[l], conv_b[l][None], b_lru_gate[l], lru_lambda[l][None]], axis=0)
        o_rnn = _rglru(zr, prm, w_lru_gate[l], batch, seq)

        mixed = _mix(o_attn, o_rnn, w_branch, l, zr)
        xf = _out_proj(mixed, w_out, l, xf, gate1, seq)

        xf = _moe(xf, g_norm2[l], scale2, shift2, gate2, w_router[l], b_router[l], w1_bf, b1_e, w2_bf, b2_e, l,
                  g_final, seq, final_norm=(l == depth - 1))
    return xf.reshape(batch, seq, d)
```

```python
import functools
import math

import numpy as np
import jax
import jax.numpy as jnp
from jax import lax
from jax.experimental import pallas as pl
from jax.experimental.pallas import tpu as pltpu

F32 = jnp.float32
BF16 = jnp.bfloat16

D_MODEL = 4096
HEAD_DIM = 128
N_Q_HEADS = 16
N_KV_HEADS = 4
HEADS_PER_GROUP = 4
Q_WIDTH = N_Q_HEADS * HEAD_DIM
KV_WIDTH = N_KV_HEADS * HEAD_DIM
ATTN_SCALE = HEAD_DIM ** -0.5
ROPE_THETA = 500000.0
ROPE_DIM = 32
CMP_BLOCK = 32
CMP_STRIDE = 16
CMP_HIDDEN = 256
SEL_BLOCK = 64
N_SELECT = 16
N_LOCAL_SEL = 2
WINDOW = 512
RNN_WIDTH = 2048
RNN_HEADS = 16
CONV_WIDTH = 4
LRU_C = 8.0
N_EXPERTS = 32
TOP_K = 4
EXPERT_FF = 512
SWIGLU_LIMIT = 7.0
SWIGLU_ALPHA = 1.702
N_ADA = 6
NORM_EPS = 1e-6
NEG_INF = -1e30
FORCED_SCORE = 1e6

SEG_A = Q_WIDTH + 6 * KV_WIDTH
GATE_W = 3 * N_Q_HEADS
SEG_R0 = SEG_A + GATE_W
SEG_R = 2 * RNN_WIDTH + 2 * D_MODEL

LANES = 128
VMEM_LIMIT_MB = 56
MM_TM = 1024
MM_TN = 512
ATT_TQ = 256
ATT_TK = 256
RNN_TT = 256
RNN_CB = 256
NORM_TM = 256
MOE_TM = 256
COMB_TT = 128


def _cparams(sem):
    return pltpu.CompilerParams(dimension_semantics=sem, vmem_limit_bytes=VMEM_LIMIT_MB << 20)


def _sigmoid(x):
    return 1.0 / (1.0 + jnp.exp(-x))


def _gelu_tanh(x):
    return 0.5 * x * (1.0 + jnp.tanh(0.7978845608028654 * (x + 0.044715 * (x * x * x))))


def _bdot(a, b):
    return jnp.dot(a, b, preferred_element_type=F32)


def _split_bf16(x):
    hi = x.astype(BF16)
    lo = (x - hi.astype(F32)).astype(BF16)
    return hi, lo


def _ada_kernel(c_ref, w_ref, b_ref, t_ref, o_ref):
    c = c_ref[...]
    s = (c * _sigmoid(c)).astype(BF16)
    emb = _bdot(s, w_ref[...].astype(BF16)) + b_ref[...]
    for l in range(o_ref.shape[0]):
        o_ref[l] = emb + t_ref[l]


def _ada_mod(c, w_ada, b_ada, ada_table):
    depth = ada_table.shape[0]
    b = c.shape[0]
    n = w_ada.shape[1]
    c8 = jnp.zeros((8, D_MODEL), F32).at[:b].set(c)
    tn = 512
    return pl.pallas_call(
        _ada_kernel,
        grid=(n // tn,),
        in_specs=[
            pl.BlockSpec((8, D_MODEL), lambda j: (0, 0)),
            pl.BlockSpec((D_MODEL, tn), lambda j: (0, j)),
            pl.BlockSpec((1, tn), lambda j: (0, j)),
            pl.BlockSpec((depth, 1, tn), lambda j: (0, 0, j)),
        ],
        out_specs=pl.BlockSpec((depth, 8, tn), lambda j: (0, 0, j)),
        out_shape=jax.ShapeDtypeStruct((depth, 8, n), F32),
        compiler_params=_cparams(("arbitrary",)),
        name="ada_mod",
    )(c8, w_ada, b_ada.reshape(1, n), ada_table.reshape(depth, 1, n))


def _norm_mod(x, g, sc, sh):
    r = lax.rsqrt(jnp.mean(x * x, axis=-1, keepdims=True) + NORM_EPS)
    return ((x * r) * g) * (1.0 + sc) + sh


def _norm_kernel(x_ref, g_ref, sc_ref, sh_ref, o_ref):
    o_ref[...] = _norm_mod(x_ref[...], g_ref[...], sc_ref[...], sh_ref[...]).astype(o_ref.dtype)


def _norm_call(x, g, sc, sh, seq):
    n, d = x.shape
    tm = NORM_TM
    per_b = seq // tm
    return pl.pallas_call(
        _norm_kernel,
        grid=(n // tm,),
        in_specs=[
            pl.BlockSpec((tm, d), lambda i: (i, 0)),
            pl.BlockSpec((1, d), lambda i: (0, 0)),
            pl.BlockSpec((None, 1, d), lambda i: (i // per_b, 0, 0)),
            pl.BlockSpec((None, 1, d), lambda i: (i // per_b, 0, 0)),
        ],
        out_specs=pl.BlockSpec((tm, d), lambda i: (i, 0)),
        out_shape=jax.ShapeDtypeStruct((n, d), BF16),
        compiler_params=_cparams(("arbitrary",)),
        name="norm_mod",
    )(x, g.reshape(1, d), sc, sh)


def _rope_tables(seq):
    half = ROPE_DIM // 2
    inv_freq = jnp.power(ROPE_THETA, -jnp.arange(half, dtype=F32) * 2.0 / ROPE_DIM)
    ang = jnp.arange(seq, dtype=F32)[:, None] * inv_freq[None, :]
    cos, sin = jnp.cos(ang), jnp.sin(ang)
    ones = jnp.ones((seq, HEAD_DIM - ROPE_DIM), F32)
    c_k = jnp.concatenate([cos, cos, ones], axis=1)
    s_k = jnp.concatenate([-sin, sin, 0.0 * ones], axis=1)
    c_id = jnp.ones((seq, HEAD_DIM), F32)
    s_id = jnp.zeros((seq, HEAD_DIM), F32)
    return (jnp.stack([c_k * ATTN_SCALE, c_k, c_id]), jnp.stack([s_k * ATTN_SCALE, s_k, s_id]))


def _rope_apply(x, cos, sin):
    heads = x.shape[1] // HEAD_DIM
    lane = lax.broadcasted_iota(jnp.int32, x.shape, 1) & (HEAD_DIM - 1)
    half = ROPE_DIM // 2
    fwd = pltpu.roll(x, half, 1)
    bwd = pltpu.roll(x, x.shape[1] - half, 1)
    sw = jnp.where(lane < half, bwd, fwd)
    if heads > 1:
        cos = jnp.concatenate([cos] * heads, axis=1)
        sin = jnp.concatenate([sin] * heads, axis=1)
    return x * cos + sw * sin


def _dot_nt(a, wt):
    return lax.dot_general(a, wt.astype(BF16), (((1,), (1,)), ((), ())), preferred_element_type=F32)


EPI_CHUNKS = 4


def _proj_rope_kernel(a_ref, w_ref, cos_ref, sin_ref, o_ref):
    w = w_ref[...].astype(BF16)
    rc = a_ref.shape[0] // EPI_CHUNKS
    for r in range(EPI_CHUNKS):
        sl = slice(r * rc, (r + 1) * rc)
        acc = _dot_nt(a_ref[sl, :], w)
        o_ref[sl, :] = _rope_apply(acc, cos_ref[sl, :], sin_ref[sl, :]).astype(o_ref.dtype)


def _proj_kernel(a_ref, w_ref, o_ref):
    w = w_ref[0] if len(w_ref.shape) == 3 else w_ref[...]
    o_ref[...] = _dot_nt(a_ref[...], w).astype(o_ref.dtype)


def _proj_a(h, w_in, layer, cos_t, sin_t, seq):
    n, d = h.shape
    tm, tn = MM_TM, MM_TN
    per_b = seq // tm

    def variant(j):
        return jnp.where(j < 4, 0, jnp.where((j == 6) | (j == 8), 1, 2))

    return pl.pallas_call(
        _proj_rope_kernel,
        grid=(n // tm, SEG_A // tn),
        in_specs=[
            pl.BlockSpec((tm, d), lambda i, j: (i, 0)),
            pl.BlockSpec((None, tn, d), lambda i, j: (layer, j, 0)),
            pl.BlockSpec((None, tm, HEAD_DIM), lambda i, j: (variant(j), i % per_b, 0)),
            pl.BlockSpec((None, tm, HEAD_DIM), lambda i, j: (variant(j), i % per_b, 0)),
        ],
        out_specs=pl.BlockSpec((tm, tn), lambda i, j: (i, j)),
        out_shape=jax.ShapeDtypeStruct((n, SEG_A), BF16),
        compiler_params=_cparams(("arbitrary", "arbitrary")),
        name="proj_qkv",
    )(h, w_in, cos_t, sin_t)


def _proj_gate(h, w_in, layer):
    n, d = h.shape
    tm = MM_TM
    return pl.pallas_call(
        _proj_kernel,
        grid=(n // tm,),
        in_specs=[
            pl.BlockSpec((tm, d), lambda i: (i, 0)),
            pl.BlockSpec((None, LANES, d), lambda i: (layer, SEG_A // LANES, 0)),
        ],
        out_specs=pl.BlockSpec((tm, LANES), lambda i: (i, 0)),
        out_shape=jax.ShapeDtypeStruct((n, LANES), F32),
        compiler_params=_cparams(("arbitrary",)),
        name="proj_gate",
    )(h, w_in)


def _proj_r(h, w_in, layer):
    n, d = h.shape
    tm, tn = MM_TM, MM_TN
    return pl.pallas_call(
        _proj_kernel,
        grid=(n // tm, SEG_R // tn),
        in_specs=[
            pl.BlockSpec((tm, d), lambda i, j: (i, 0)),
            pl.BlockSpec((pl.Element(1), pl.Element(tn), pl.Element(d)), lambda i, j: (layer, pl.multiple_of(SEG_R0 + j * tn, 8), 0)),
        ],
        out_specs=pl.BlockSpec((tm, tn), lambda i, j: (i, j)),
        out_shape=jax.ShapeDtypeStruct((n, SEG_R), F32),
        compiler_params=_cparams(("arbitrary", "arbitrary")),
        name="proj_rnn_mix",
    )(h, w_in)


def _cmp_kernel(z_ref, w1_ref, w2_ref, pos_ref, cos_ref, sin_ref, o_ref):
    z = z_ref[...]
    w1 = w1_ref[...].astype(BF16)
    half = w1.shape[0] // 2
    a = _bdot(z, w1[:half])
    b = _bdot(z, w1[half:])
    pos = jnp.broadcast_to(pos_ref[...], (8, pos_ref.shape[1])).astype(BF16)
    bias = _bdot(pos, w1)[0:1]
    y = a + pltpu.roll(b, b.shape[0] - 1, 0) + bias
    hid = _gelu_tanh(y).astype(BF16)
    out = _bdot(hid, w2_ref[...].astype(BF16))
    groups = out.shape[0] // cos_ref.shape[0]
    cos = jnp.concatenate([cos_ref[...]] * groups, axis=0)
    sin = jnp.concatenate([sin_ref[...]] * groups, axis=0)
    o_ref[...] = _rope_apply(out, cos, sin).astype(o_ref.dtype)


def _compress(z2, w_cmp1, w_cmp2, cmp_pos, layer, cos_c, sin_c):
    _, b, rows, width = z2.shape
    n_grp = cos_c.shape[1]
    return pl.pallas_call(
        _cmp_kernel,
        grid=(2, b),
        in_specs=[
            pl.BlockSpec((None, None, rows, width), lambda j, bi: (j, bi, 0, 0)),
            pl.BlockSpec((None, None, 2 * width, CMP_HIDDEN), lambda j, bi: (layer, j, 0, 0)),
            pl.BlockSpec((None, None, CMP_HIDDEN, HEAD_DIM), lambda j, bi: (layer, j, 0, 0)),
            pl.BlockSpec((None, None, 1, 2 * width), lambda j, bi: (layer, j, 0, 0)),
            pl.BlockSpec((None, n_grp, HEAD_DIM), lambda j, bi: (j, 0, 0)),
            pl.BlockSpec((None, n_grp, HEAD_DIM), lambda j, bi: (j, 0, 0)),
        ],
        out_specs=pl.BlockSpec((None, None, rows, HEAD_DIM), lambda j, bi: (j, bi, 0, 0)),
        out_shape=jax.ShapeDtypeStruct((2, b, rows, HEAD_DIM), BF16),
        compiler_params=_cparams(("arbitrary", "arbitrary")),
        name="nsa_compress",
    )(z2, w_cmp1, w_cmp2, cmp_pos.reshape(cmp_pos.shape[0], 2, 1, CMP_BLOCK * HEAD_DIM), cos_c, sin_c)


def _attn_t_kernel(q_ref, kc_ref, vc_ref, ks_ref, vst_ref, kw_ref, vwt_ref, gl_ref, ovt_ref, ext_ref, o_ref):
    tq, tk = ATT_TQ, ATT_TK
    hpg = HEADS_PER_GROUP
    rows = hpg * tq
    i = pl.program_id(2)
    qb = q_ref[...]
    q4 = jnp.concatenate([qb[:, h * HEAD_DIM:(h + 1) * HEAD_DIM] for h in range(hpg)], axis=0)
    dn_nt = (((1,), (1,)), ((), ()))

    def tile_heads(x):
        return jnp.concatenate([x] * hpg, axis=1)

    n_cmp = kc_ref.shape[0] - 1
    key_n = lax.broadcasted_iota(jnp.int32, (LANES, tq), 0)
    t_pos = i * tq + lax.broadcasted_iota(jnp.int32, (LANES, tq), 1)
    cmask = (key_n < n_cmp) & (key_n * CMP_STRIDE + (CMP_BLOCK - 1) <= t_pos)
    cb4 = tile_heads(jnp.where(cmask, 0.0, NEG_INF))
    s = lax.dot_general(kc_ref[...], q4, dn_nt, preferred_element_type=F32) + cb4
    m = jnp.max(s, axis=0, keepdims=True)
    e = jnp.where(cb4 > -1.0, jnp.exp(s - m), 0.0)
    den = jnp.sum(e, axis=0, keepdims=True)
    p = e / jnp.maximum(den, 1e-30)
    vc_t = jnp.transpose(vc_ref[...].astype(F32)).astype(BF16)
    o_cmp = _bdot(vc_t, p.astype(BF16))
    p_sum = p[:, 0:tq]
    for h in range(1, hpg):
        p_sum = p_sum + p[:, h * tq:(h + 1) * tq]
    p_hi, p_lo = _split_bf16(p_sum)
    n_slc = ext_ref.shape[0] * (tk // SEL_BLOCK)
    imp = (_bdot(ovt_ref[...], p_hi) + _bdot(ovt_ref[...], p_lo))[:n_slc]

    blk = lax.broadcasted_iota(jnp.int32, (n_slc, tq), 0)
    cur = lax.shift_right_logical(i * tq + lax.broadcasted_iota(jnp.int32, (n_slc, tq), 1), int(math.log2(SEL_BLOCK)))
    visible = blk <= cur
    forced = (blk == 0) | (visible & (blk > cur - N_LOCAL_SEL))
    score = jnp.where(visible, jnp.where(forced, FORCED_SCORE, imp), NEG_INF)
    rank = jnp.zeros((n_slc, tq), F32)
    for k in range(n_slc):
        row = score[k:k + 1, :]
        beats = (row > score) | ((row == score) & (blk > k))
        rank = rank + jnp.where(beats, 1.0, 0.0)
    sel = jnp.where((rank < float(min(N_SELECT, n_slc))) & visible, 1.0, 0.0)
    sel = jnp.concatenate([sel, jnp.zeros((LANES - n_slc, tq), F32)], axis=0).astype(BF16)

    rel = (lax.broadcasted_iota(jnp.int32, (tk, tq), 1) + i * tq) - lax.broadcasted_iota(jnp.int32, (tk, tq), 0)

    def flash_step(c, carry, k_ref, vt_ref, bias):
        m_p, l_p, acc_p = carry
        off = pl.multiple_of(c * tk, tk)
        kk = k_ref[pl.ds(off, tk), :]
        sc = lax.dot_general(kk, q4, dn_nt, preferred_element_type=F32) + tile_heads(bias)
        m_n = jnp.maximum(m_p, jnp.max(sc, axis=0, keepdims=True))
        alpha = jnp.exp(m_p - m_n)
        pp = jnp.exp(sc - m_n)
        l_n = alpha * l_p + jnp.sum(pp, axis=0, keepdims=True)
        acc_n = alpha * acc_p + _bdot(vt_ref[c], pp.astype(BF16))
        return m_n, l_n, acc_n

    init = (jnp.full((1, rows), NEG_INF, F32), jnp.zeros((1, rows), F32), jnp.zeros((HEAD_DIM, rows), F32))

    def sel_body(c, carry):
        kmask = _bdot(ext_ref[c], sel) > 0.5
        ok = kmask & (rel - c * tk >= 0)
        return flash_step(c, carry, ks_ref, vst_ref, jnp.where(ok, 0.0, NEG_INF))

    _, l_s, acc_s = lax.fori_loop(0, i + 1, sel_body, init)

    def win_body(c, carry):
        d = rel - c * tk
        ok = (d >= 0) & (d < WINDOW)
        return flash_step(c, carry, kw_ref, vwt_ref, jnp.where(ok, 0.0, NEG_INF))

    c0 = jnp.maximum(i - WINDOW // tk, 0)
    _, l_w, acc_w = lax.fori_loop(c0, i + 1, win_body, init)

    o_sel = acc_s / l_s
    o_win = acc_w / l_w
    gates = _sigmoid(gl_ref[...])
    outs = []
    for h in range(hpg):
        sl = slice(h * tq, (h + 1) * tq)
        o_h = (gates[3 * h:3 * h + 1] * o_cmp[:, sl] + gates[3 * h + 1:3 * h + 2] * o_sel[:, sl]
               + gates[3 * h + 2:3 * h + 3] * o_win[:, sl])
        outs.append(jnp.transpose(o_h))
    o_ref[...] = jnp.concatenate(outs, axis=1).astype(o_ref.dtype)


def _attention_t(za, kvc, vst, vwt, gl_t, ov_t, ex_t, batch, seq):
    n = za.shape[0]
    tq = ATT_TQ
    per_b = seq // tq
    g = N_KV_HEADS
    n_cb = kvc.shape[2] // g
    kv_blk = lambda col0: pl.BlockSpec((seq, HEAD_DIM), lambda b, gi, i: (b, col0 + gi))
    vt_blk = pl.BlockSpec((None, None) + vst.shape[2:], lambda b, gi, i: (b, gi, 0, 0, 0))
    q_cols = Q_WIDTH // HEAD_DIM
    return pl.pallas_call(
        _attn_t_kernel,
        grid=(batch, g, per_b),
        in_specs=[
            pl.BlockSpec((tq, HEADS_PER_GROUP * HEAD_DIM), lambda b, gi, i: (b * per_b + i, gi)),
            pl.BlockSpec((None, None, n_cb, HEAD_DIM), lambda b, gi, i: (0, b, gi, 0)),
            pl.BlockSpec((None, None, n_cb, HEAD_DIM), lambda b, gi, i: (1, b, gi, 0)),
            kv_blk(q_cols + 2 * g), vt_blk, kv_blk(q_cols + 4 * g), vt_blk,
            pl.BlockSpec((None, gl_t.shape[1], tq), lambda b, gi, i: (gi, 0, b * per_b + i)),
            pl.BlockSpec((LANES, LANES), lambda b, gi, i: (0, 0)),
            pl.BlockSpec(ex_t.shape, lambda b, gi, i: (0, 0, 0)),
        ],
        out_specs=pl.BlockSpec((tq, HEADS_PER_GROUP * HEAD_DIM), lambda b, gi, i: (b * per_b + i, gi)),
        out_shape=jax.ShapeDtypeStruct((n, Q_WIDTH), BF16),
        compiler_params=_cparams(("arbitrary", "arbitrary", "arbitrary")),
        name="nsa_attention",
    )(za, kvc, kvc, za, vst, za, vwt, gl_t, ov_t, ex_t)


def _rglru_kernel(x_ref, g_ref, p_ref, wg_ref, o_ref, xbuf, hc):
    tt, cb = x_ref.shape
    step = pl.program_id(2)

    @pl.when(step == 0)
    def _():
        xbuf[0:8, :] = jnp.zeros((8, cb), F32)
        hc[...] = jnp.zeros_like(hc)

    x = x_ref[...]
    xbuf[8:8 + tt, :] = x
    prm = p_ref[...]
    xc = prm[4:5] + prm[3:4] * x
    for k in range(CONV_WIDTH - 1):
        shift = CONV_WIDTH - 1 - k
        xc = xc + prm[k:k + 1] * xbuf[8 - shift:8 - shift + tt, :]
    xbuf[0:8, :] = x[tt - 8:tt]

    xcb = xc.astype(BF16)
    pre_r, pre_i = [], []
    for h in range(cb // HEAD_DIM):
        xh = xcb[:, h * HEAD_DIM:(h + 1) * HEAD_DIM]
        pre_r.append(_bdot(xh, wg_ref[0, h].astype(BF16)))
        pre_i.append(_bdot(xh, wg_ref[1, h].astype(BF16)))
    r = _sigmoid(jnp.concatenate(pre_r, axis=1) + prm[5:6])
    ig = _sigmoid(jnp.concatenate(pre_i, axis=1) + prm[6:7])
    zl = -prm[7:8]
    softplus = jnp.maximum(zl, 0.0) + jnp.log1p(jnp.exp(-jnp.abs(zl)))
    log_a = (-LRU_C) * r * softplus
    a = jnp.exp(log_a)
    b = jnp.sqrt(1.0 - a * a) * (ig * xc)

    row = lax.broadcasted_iota(jnp.int32, (tt, cb), 0)
    s = 1
    while s < tt:
        keep = row >= s
        a_sh = jnp.where(keep, pltpu.roll(a, s, 0), 1.0)
        b_sh = jnp.where(keep, pltpu.roll(b, s, 0), 0.0)
        b = a * b_sh + b
        a = a * a_sh
        s *= 2
    hcur = b + a * hc[0:1, :]
    hc[...] = jnp.broadcast_to(hcur[tt - 1:tt], hc.shape)
    o_ref[...] = (hcur * _gelu_tanh(g_ref[...])).astype(o_ref.dtype)


def _rglru(zr, prm, w_gate, batch, seq):
    n = zr.shape[0]
    tt, cb = RNN_TT, RNN_CB
    per_b = seq // tt
    n_cb = RNN_WIDTH // cb
    hpb = cb // HEAD_DIM
    return pl.pallas_call(
        _rglru_kernel,
        grid=(batch, n_cb, per_b),
        in_specs=[
            pl.BlockSpec((tt, cb), lambda b, c, t: (b * per_b + t, c)),
            pl.BlockSpec((tt, cb), lambda b, c, t: (b * per_b + t, n_cb + c)),
            pl.BlockSpec((8, cb), lambda b, c, t: (0, c)),
            pl.BlockSpec((2, hpb, HEAD_DIM, HEAD_DIM), lambda b, c, t: (0, c, 0, 0)),
        ],
        out_specs=pl.BlockSpec((tt, cb), lambda b, c, t: (b * per_b + t, c)),
        out_shape=jax.ShapeDtypeStruct((n, RNN_WIDTH), BF16),
        scratch_shapes=[pltpu.VMEM((tt + 8, cb), F32), pltpu.VMEM((8, cb), F32)],
        compiler_params=_cparams(("arbitrary", "arbitrary", "arbitrary")),
        name="rglru",
    )(zr, zr, prm, w_gate)


def _mix_kernel(a1_ref, a2_ref, w1_ref, w2_ref, m1_ref, m2_ref, o_ref):
    w1 = w1_ref[...].astype(BF16)
    w2 = w2_ref[...].astype(BF16)
    rc = a1_ref.shape[0] // EPI_CHUNKS
    for r in range(EPI_CHUNKS):
        sl = slice(r * rc, (r + 1) * rc)
        y1 = _bdot(a1_ref[sl, :], w1)
        y2 = _bdot(a2_ref[sl, :], w2)
        o_ref[sl, :] = (_sigmoid(m1_ref[sl, :]) * y1 + _sigmoid(m2_ref[sl, :]) * y2).astype(o_ref.dtype)


def _mix(o_attn, o_rnn, w_branch, layer, zr):
    n, k = o_attn.shape
    tm, tn = MM_TM, MM_TN
    m_off = 2 * RNN_WIDTH // tn
    return pl.pallas_call(
        _mix_kernel,
        grid=(n // tm, D_MODEL // tn),
        in_specs=[
            pl.BlockSpec((tm, k), lambda i, j: (i, 0)),
            pl.BlockSpec((tm, k), lambda i, j: (i, 0)),
            pl.BlockSpec((None, None, k, tn), lambda i, j: (layer, 0, 0, j)),
            pl.BlockSpec((None, None, k, tn), lambda i, j: (layer, 1, 0, j)),
            pl.BlockSpec((tm, tn), lambda i, j: (i, m_off + j)),
            pl.BlockSpec((tm, tn), lambda i, j: (i, m_off + D_MODEL // tn + j)),
        ],
        out_specs=pl.BlockSpec((tm, tn), lambda i, j: (i, j)),
        out_shape=jax.ShapeDtypeStruct((n, D_MODEL), BF16),
        compiler_params=_cparams(("arbitrary", "arbitrary")),
        name="mixer_merge",
    )(o_attn, o_rnn, w_branch, w_branch, zr, zr)


def _out_kernel(a_ref, w_ref, x_ref, g_ref, o_ref):
    o_ref[...] = x_ref[...] + g_ref[...] * _bdot(a_ref[...], w_ref[...].astype(BF16))


def _out_proj(mixed, w_out, layer, x, gate, seq):
    n, d = x.shape
    tm, tn = MM_TM, MM_TN
    per_b = seq // tm
    return pl.pallas_call(
        _out_kernel,
        grid=(n // tm, d // tn),
        in_specs=[
            pl.BlockSpec((tm, d), lambda i, j: (i, 0)),
            pl.BlockSpec((None, d, tn), lambda i, j: (layer, 0, j)),
            pl.BlockSpec((tm, tn), lambda i, j: (i, j)),
            pl.BlockSpec((None, 1, tn), lambda i, j: (i // per_b, 0, j)),
        ],
        out_specs=pl.BlockSpec((tm, tn), lambda i, j: (i, j)),
        out_shape=jax.ShapeDtypeStruct((n, d), F32),
        compiler_params=_cparams(("arbitrary", "arbitrary")),
        name="out_proj",
    )(mixed, w_out, x, gate)


def _route_kernel(x_ref, g_ref, sc_ref, sh_ref, wr_ref, br_ref, h_ref, meta_ref, cnt_ref, run):
    tm = x_ref.shape[0]

    @pl.when(pl.program_id(0) == 0)
    def _():
        run[...] = jnp.zeros_like(run)

    h = _norm_mod(x_ref[...], g_ref[...], sc_ref[...], sh_ref[...])
    h_ref[...] = h
    h_hi, h_lo = _split_bf16(h)
    w_hi, w_lo = _split_bf16(wr_ref[...])
    logits = _bdot(h_hi, w_hi) + _bdot(h_hi, w_lo) + _bdot(h_lo, w_hi) + br_ref[...]
    lane = lax.broadcasted_iota(jnp.int32, (tm, LANES), 1)
    lane_f = lane.astype(F32)
    cur = jnp.where(lane < N_EXPERTS, logits, -jnp.inf)
    vals, idxs = [], []
    for _ in range(TOP_K):
        mx = jnp.max(cur, axis=-1, keepdims=True)
        ix = jnp.min(jnp.where(cur == mx, lane_f, float(LANES)), axis=-1, keepdims=True)
        vals.append(mx)
        idxs.append(ix)
        cur = jnp.where(lane_f == ix, -jnp.inf, cur)
    exps = [jnp.exp(v - vals[0]) for v in vals]
    den = exps[0]
    for e in exps[1:]:
        den = den + e
    hot = [lane_f == ix for ix in idxs]
    cnt = jnp.zeros((tm, LANES), F32)
    for hk in hot:
        cnt = cnt + jnp.where(hk, 1.0, 0.0)
    tri = jnp.where(lax.broadcasted_iota(jnp.int32, (tm, tm), 0) > lax.broadcasted_iota(jnp.int32, (tm, tm), 1),
                    1.0, 0.0).astype(BF16)
    pos = _bdot(tri, cnt.astype(BF16)) + run[0:1, :]
    meta = jnp.zeros((tm, LANES), F32)
    for k in range(TOP_K):
        pk = jnp.sum(jnp.where(hot[k], pos, 0.0), axis=-1, keepdims=True)
        meta = jnp.where(lane == k, idxs[k], meta)
        meta = jnp.where(lane == TOP_K + k, pk, meta)
        meta = jnp.where(lane == 2 * TOP_K + k, exps[k] / den, meta)
    meta_ref[...] = meta
    run[...] = run[...] + jnp.sum(cnt, axis=0, keepdims=True)
    cnt_ref[...] = run[...]


def _route(x, g, sc, sh, w_router, b_router, seq):
    n, d = x.shape
    tm = NORM_TM
    per_b = seq // tm
    wr = jnp.zeros((d, LANES), F32).at[:, :N_EXPERTS].set(w_router)
    br = jnp.zeros((1, LANES), F32).at[0, :N_EXPERTS].set(b_router)
    return pl.pallas_call(
        _route_kernel,
        grid=(n // tm,),
        in_specs=[
            pl.BlockSpec((tm, d), lambda i: (i, 0)),
            pl.BlockSpec((1, d), lambda i: (0, 0)),
            pl.BlockSpec((None, 1, d), lambda i: (i // per_b, 0, 0)),
            pl.BlockSpec((None, 1, d), lambda i: (i // per_b, 0, 0)),
            pl.BlockSpec((d, LANES), lambda i: (0, 0)),
            pl.BlockSpec((1, LANES), lambda i: (0, 0)),
        ],
        out_specs=[
            pl.BlockSpec((tm, d), lambda i: (i, 0)),
            pl.BlockSpec((tm, LANES), lambda i: (i, 0)),
            pl.BlockSpec((8, LANES), lambda i: (0, 0)),
        ],
        out_shape=[
            jax.ShapeDtypeStruct((n, d), F32),
            jax.ShapeDtypeStruct((n, LANES), F32),
            jax.ShapeDtypeStruct((8, LANES), F32),
        ],
        scratch_shapes=[pltpu.VMEM((8, LANES), F32)],
        compiler_params=_cparams(("arbitrary",)),
        name="moe_route",
    )(x, g.reshape(1, d), sc, sh, wr, br)


def _expert_kernel(te_ref, nt_ref, tok_ref, h_hbm, w1_ref, b1_ref, w2_ref, b2_ref, even_ref, o_ref, xbuf, sem):
    tm = o_ref.shape[0]
    j = pl.program_id(0)
    nt = nt_ref[0]
    cur = j % 2

    def start_gather(tile, buf):
        base = tile * tm

        def body(ro, c):
            r0 = pl.multiple_of(ro * 8, 8)
            for ri in range(8):
                tok = tok_ref[base + r0 + ri]
                pltpu.make_async_copy(h_hbm.at[pl.ds(tok, 1), :], xbuf.at[buf, pl.ds(r0 + ri, 1), :],
                                      sem.at[buf]).start()
            return c
        lax.fori_loop(0, tm // 8, body, 0)

    def wait_gather(buf):
        pltpu.make_async_copy(h_hbm.at[pl.ds(0, tm), :], xbuf.at[buf], sem.at[buf]).wait()

    @pl.when(j == 0)
    def _():
        start_gather(0, 0)

    @pl.when(j < nt)
    def _():
        wait_gather(cur)
        x = xbuf[cur].astype(BF16)
        z = _bdot(x, w1_ref[...]) + b1_ref[...]
        z_next = pltpu.roll(z, z.shape[1] - 1, 1)
        glu = jnp.minimum(z, SWIGLU_LIMIT)
        lin = jnp.clip(z_next, -SWIGLU_LIMIT, SWIGLU_LIMIT)
        act = (glu * _sigmoid(SWIGLU_ALPHA * glu) * (lin + 1.0)).astype(BF16)
        act = _bdot(act, even_ref[...]).astype(BF16)
        o_ref[...] = _bdot(act, w2_ref[...]) + b2_ref[...]
        nxt = jnp.minimum(j + 1, nt - 1) * tm
        for r in range(tm):
            tok = tok_ref[nxt + r]
            pltpu.make_async_copy(h_hbm.at[pl.ds(tok, 1), :], xbuf.at[1 - cur, pl.ds(r, 1), :],
                                  sem.at[1 - cur]).start()

    @pl.when(j == nt)
    def _():
        wait_gather(cur)

    @pl.when(j >= nt)
    def _():
        o_ref[...] = jnp.zeros_like(o_ref)


def _experts(h2, tile_expert, n_tiles, tok, w1, b1, w2, b2, layer):
    n, d = h2.shape
    tm = MOE_TM
    max_tiles = tile_expert.shape[0]
    ff2 = w1.shape[3]
    ff = ff2 // 2
    even = np.zeros((ff2, ff), np.float32)
    even[2 * np.arange(ff), np.arange(ff)] = 1.0
    e_map = lambda j, te, nt, tk: (layer, te[j], 0, 0)
    return pl.pallas_call(
        _expert_kernel,
        grid_spec=pltpu.PrefetchScalarGridSpec(
            num_scalar_prefetch=3,
            grid=(max_tiles,),
            in_specs=[
                pl.BlockSpec(memory_space=pl.ANY),
                pl.BlockSpec((None, None, d, ff2), e_map),
                pl.BlockSpec((None, None, 1, ff2), e_map),
                pl.BlockSpec((None, None, ff, d), e_map),
                pl.BlockSpec((None, None, 1, d), e_map),
                pl.BlockSpec((ff2, ff), lambda j, te, nt, tk: (0, 0)),
            ],
            out_specs=pl.BlockSpec((tm, d), lambda j, te, nt, tk: (j, 0)),
            scratch_shapes=[pltpu.VMEM((2, tm, d), F32), pltpu.SemaphoreType.DMA((2,))],
        ),
        out_shape=jax.ShapeDtypeStruct((max_tiles * tm, d), F32),
        compiler_params=_cparams(("arbitrary",)),
        name="moe_experts",
    )(tile_expert, n_tiles, tok, h2, w1, b1, w2, b2, jnp.asarray(even, BF16))


def _combine_kernel(slot_ref, y_hbm, x_ref, w_ref, g_ref, gf_ref, o_ref, ybuf, sem, *, final_norm):
    tt = x_ref.shape[0]
    i = pl.program_id(0)
    nsteps = pl.num_programs(0)
    cur = i % 2

    def start_gather(step, buf):
        base = step * (tt * TOP_K)

        def body(ro, c):
            r0 = pl.multiple_of(ro * 8, 8)
            for ri in range(8):
                for k in range(TOP_K):
                    s = slot_ref[base + (r0 + ri) * TOP_K + k]
                    pltpu.make_async_copy(y_hbm.at[pl.ds(s, 1), :], ybuf.at[buf, k, pl.ds(r0 + ri, 1), :],
                                          sem.at[buf]).start()
            return c
        lax.fori_loop(0, tt // 8, body, 0)

    @pl.when(i == 0)
    def _():
        start_gather(0, 0)

    @pl.when(i + 1 < nsteps)
    def _():
        start_gather(i + 1, 1 - cur)

    for k in range(TOP_K):
        pltpu.make_async_copy(y_hbm.at[pl.ds(0, tt), :], ybuf.at[cur, k], sem.at[cur]).wait()
    w = w_ref[...]
    y = w[:, 0:1] * ybuf[cur, 0]
    for k in range(1, TOP_K):
        y = y + w[:, k:k + 1] * ybuf[cur, k]
    out = x_ref[...] + g_ref[...] * y
    if final_norm:
        r = lax.rsqrt(jnp.mean(out * out, axis=-1, keepdims=True) + NORM_EPS)
        out = (out * r) * gf_ref[...]
    o_ref[...] = out


def _combine(slots, y_slots, x, wts, gate, g_final, seq, final_norm):
    n, d = x.shape
    tt = COMB_TT
    per_b = seq // tt
    return pl.pallas_call(
        functools.partial(_combine_kernel, final_norm=final_norm),
        grid_spec=pltpu.PrefetchScalarGridSpec(
            num_scalar_prefetch=1,
            grid=(n // tt,),
            in_specs=[
                pl.BlockSpec(memory_space=pl.ANY),
                pl.BlockSpec((tt, d), lambda i, s: (i, 0)),
                pl.BlockSpec((tt, TOP_K), lambda i, s: (i, 0)),
                pl.BlockSpec((None, 1, d), lambda i, s: (i // per_b, 0, 0)),
                pl.BlockSpec((1, d), lambda i, s: (0, 0)),
            ],
            out_specs=pl.BlockSpec((tt, d), lambda i, s: (i, 0)),
            scratch_shapes=[pltpu.VMEM((2, TOP_K, tt, d), F32), pltpu.SemaphoreType.DMA((2,))],
        ),
        out_shape=jax.ShapeDtypeStruct((n, d), F32),
        compiler_params=_cparams(("arbitrary",)),
        name="moe_combine",
    )(slots, y_slots, x, wts, gate, g_final.reshape(1, d))


def _moe(x, g2, sc, sh, gate, w_router, b_router, w1, b1, w2, b2, layer, g_final, seq, final_norm):
    n, d = x.shape
    tm = MOE_TM
    max_tiles = n * TOP_K // tm + N_EXPERTS
    assert (n * TOP_K + N_EXPERTS * (tm - 1)) // tm < max_tiles
    h2, meta, cnt = _route(x, g2, sc, sh, w_router, b_router, seq)

    idx = meta[:, 0:TOP_K].astype(jnp.int32)
    pos = meta[:, TOP_K:2 * TOP_K].astype(jnp.int32)
    wts = meta[:, 2 * TOP_K:3 * TOP_K]
    counts = cnt[0, :N_EXPERTS].astype(jnp.int32)
    tiles_e = (counts + tm - 1) // tm
    tile_end = jnp.cumsum(tiles_e)
    offsets = (tile_end - tiles_e) * tm
    n_tiles = tile_end[-1:]
    tile_ids = jnp.arange(max_tiles, dtype=jnp.int32)
    tile_expert = jnp.minimum(jnp.sum((tile_end[None, :] <= tile_ids[:, None]).astype(jnp.int32), axis=1),
                              N_EXPERTS - 1)
    slot = offsets[idx] + pos
    tok_ids = jnp.broadcast_to(jnp.arange(n, dtype=jnp.int32)[:, None], slot.shape)
    tok = jnp.zeros((max_tiles * tm,), jnp.int32).at[slot.reshape(-1)].set(tok_ids.reshape(-1))

    y_slots = _experts(h2, tile_expert, n_tiles, tok, w1, b1, w2, b2, layer)
    return _combine(slot.reshape(-1), y_slots, x, wts, gate, g_final, seq, final_norm)


def _overlap_matrix(seq):
    n_cmp = (seq - CMP_BLOCK) // CMP_STRIDE + 1
    n_slc = seq // SEL_BLOCK
    cmp_start = np.arange(n_cmp) * CMP_STRIDE
    slc_start = np.arange(n_slc) * SEL_BLOCK
    ov = np.clip(np.minimum(cmp_start[:, None] + CMP_BLOCK, slc_start[None, :] + SEL_BLOCK)
                 - np.maximum(cmp_start[:, None], slc_start[None, :]), 0, None) / CMP_BLOCK
    out = np.zeros((LANES, LANES), np.float32)
    out[:n_slc, :n_cmp] = ov.T
    return jnp.asarray(out, BF16)


def _expand_matrix(seq):
    n_chunks = seq // ATT_TK
    key = np.arange(seq).reshape(n_chunks, ATT_TK, 1)
    blk = np.arange(LANES).reshape(1, 1, LANES)
    return jnp.asarray((key // SEL_BLOCK == blk).astype(np.float32), BF16)


def _cmp_rope_tables(seq):
    n_cmp = (seq - CMP_BLOCK) // CMP_STRIDE + 1
    half = ROPE_DIM // 2
    inv_freq = jnp.power(ROPE_THETA, -jnp.arange(half, dtype=F32) * 2.0 / ROPE_DIM)
    pos = jnp.arange(n_cmp + 1, dtype=F32) * CMP_STRIDE + (CMP_BLOCK - 1)
    ang = pos[:, None] * inv_freq[None, :]
    cos, sin = jnp.cos(ang), jnp.sin(ang)
    ones = jnp.ones((n_cmp + 1, HEAD_DIM - ROPE_DIM), F32)
    c_k = jnp.concatenate([cos, cos, ones], axis=1)
    s_k = jnp.concatenate([-sin, sin, 0.0 * ones], axis=1)
    return jnp.stack([c_k, jnp.ones_like(c_k)]), jnp.stack([s_k, jnp.zeros_like(s_k)])


def kernel(x, c, w_ada, b_ada, ada_table, g_norm1, w_in, w_cmp1, w_cmp2, cmp_pos, conv_w, conv_b,
           w_lru_gate, b_lru_gate, lru_lambda, w_branch, w_out, g_norm2, w_router, b_router,
           w_exp1, b_exp1, w_exp2, b_exp2, g_final):
    batch, seq, d = x.shape
    depth = w_in.shape[0]
    n = batch * seq
    g = N_KV_HEADS
    assert d == D_MODEL and seq % MM_TM == 0 and seq % ATT_TQ == 0 and n % COMB_TT == 0

    mod = _ada_mod(c, w_ada, b_ada, ada_table)
    cos_t, sin_t = _rope_tables(seq)
    cos_c, sin_c = _cmp_rope_tables(seq)
    ov_t = _overlap_matrix(seq)
    ex_t = _expand_matrix(seq)
    w_in = jnp.swapaxes(w_in, 1, 2)
    w1_bf = w_exp1.astype(BF16)
    w2_bf = w_exp2.astype(BF16)
    b1_e = b_exp1[:, :, None, :]
    b2_e = b_exp2[:, :, None, :]

    xf = x.reshape(n, d)
    for l in range(depth):
        m = [mod[l, :batch, j * d:(j + 1) * d].reshape(batch, 1, d) for j in range(N_ADA)]
        shift1, scale1, gate1, shift2, scale2, gate2 = m

        h = _norm_call(xf, g_norm1[l], scale1, shift1, seq)
        za = _proj_a(h, w_in, l, cos_t, sin_t, seq)
        zg = _proj_gate(h, w_in, l)
        zr = _proj_r(h, w_in, l)

        kv = za[:, Q_WIDTH:Q_WIDTH + 2 * KV_WIDTH].reshape(batch, seq // CMP_STRIDE, CMP_STRIDE, 2, g, HEAD_DIM)
        z2 = kv.transpose(3, 0, 4, 1, 2, 5).reshape(2, batch, g * (seq // CMP_STRIDE), CMP_STRIDE * HEAD_DIM)
        kvc = _compress(z2, w_cmp1, w_cmp2, cmp_pos, l, cos_c, sin_c)
        gl_t = zg[:, :GATE_W].reshape(n, g, 3 * HEADS_PER_GROUP).transpose(1, 2, 0)
        gl_t = jnp.pad(gl_t, ((0, 0), (0, 16 - 3 * HEADS_PER_GROUP), (0, 0)))

        def values_t(col0):
            v = za[:, col0:col0 + KV_WIDTH].reshape(batch, seq // ATT_TK, ATT_TK, g, HEAD_DIM)
            return v.transpose(0, 3, 1, 4, 2)

        vst = values_t(Q_WIDTH + 3 * KV_WIDTH)
        vwt = values_t(Q_WIDTH + 5 * KV_WIDTH)
        o_attn = _attention_t(za, kvc, vst, vwt, gl_t, ov_t, ex_t, batch, seq)

        prm = jnp.concatenate([conv_w[l], conv_b[l][None], b_lru_gate[l], lru_lambda[l][None]], axis=0)
        o_rnn = _rglru(zr, prm, w_lru_gate[l], batch, seq)

        mixed = _mix(o_attn, o_rnn, w_branch, l, zr)
        xf = _out_proj(mixed, w_out, l, xf, gate1, seq)

        xf = _moe(xf, g_norm2[l], scale2, shift2, gate2, w_router[l], b_router[l], w1_bf, b1_e, w2_bf, b2_e, l,
                  g_final, seq, final_norm=(l == depth - 1))
    return xf.reshape(batch, seq, d)
```

```python
import functools
import math

import numpy as np
import jax
import jax.numpy as jnp
from jax import lax
from jax.experimental import pallas as pl
from jax.experimental.pallas import tpu as pltpu

F32 = jnp.float32
BF16 = jnp.bfloat16

D_MODEL = 4096
HEAD_DIM = 128
N_Q_HEADS = 16
N_KV_HEADS = 4
HEADS_PER_GROUP = 4
Q_WIDTH = N_Q_HEADS * HEAD_DIM
KV_WIDTH = N_KV_HEADS * HEAD_DIM
ATTN_SCALE = HEAD_DIM ** -0.5
ROPE_THETA = 500000.0
ROPE_DIM = 32
CMP_BLOCK = 32
CMP_STRIDE = 16
CMP_HIDDEN = 256
SEL_BLOCK = 64
N_SELECT = 16
N_LOCAL_SEL = 2
WINDOW = 512
RNN_WIDTH = 2048
RNN_HEADS = 16
CONV_WIDTH = 4
LRU_C = 8.0
N_EXPERTS = 32
TOP_K = 4
EXPERT_FF = 512
SWIGLU_LIMIT = 7.0
SWIGLU_ALPHA = 1.702
N_ADA = 6
NORM_EPS = 1e-6
NEG_INF = -1e30
FORCED_SCORE = 1e6

SEG_A = Q_WIDTH + 6 * KV_WIDTH
GATE_W = 3 * N_Q_HEADS
SEG_R0 = SEG_A + GATE_W
SEG_R = 2 * RNN_WIDTH + 2 * D_MODEL

LANES = 128
VMEM_LIMIT_MB = 56
MM_TM = 1024
MM_TN = 512
ATT_TQ = 256
ATT_TK = 256
RNN_TT = 256
RNN_CB = 256
NORM_TM = 256
MOE_TM = 256
MOE_GATHER_BUFS = 3
COMB_TT = 128


def _cparams(sem):
    return pltpu.CompilerParams(dimension_semantics=sem, vmem_limit_bytes=VMEM_LIMIT_MB << 20)


def _sigmoid(x):
    return 1.0 / (1.0 + jnp.exp(-x))


def _gelu_tanh(x):
    return 0.5 * x * (1.0 + jnp.tanh(0.7978845608028654 * (x + 0.044715 * (x * x * x))))


def _bdot(a, b):
    return jnp.dot(a, b, preferred_element_type=F32)


def _split_bf16(x):
    hi = x.astype(BF16)
    lo = (x - hi.astype(F32)).astype(BF16)
    return hi, lo


def _ada_kernel(c_ref, w_ref, b_ref, t_ref, o_ref):
    c = c_ref[...]
    s = (c * _sigmoid(c)).astype(BF16)
    emb = _bdot(s, w_ref[...].astype(BF16)) + b_ref[...]
    for l in range(o_ref.shape[0]):
        o_ref[l] = emb + t_ref[l]


def _ada_mod(c, w_ada, b_ada, ada_table):
    depth = ada_table.shape[0]
    b = c.shape[0]
    n = w_ada.shape[1]
    c8 = jnp.zeros((8, D_MODEL), F32).at[:b].set(c)
    tn = 512
    return pl.pallas_call(
        _ada_kernel,
        grid=(n // tn,),
        in_specs=[
            pl.BlockSpec((8, D_MODEL), lambda j: (0, 0)),
            pl.BlockSpec((D_MODEL, tn), lambda j: (0, j)),
            pl.BlockSpec((1, tn), lambda j: (0, j)),
            pl.BlockSpec((depth, 1, tn), lambda j: (0, 0, j)),
        ],
        out_specs=pl.BlockSpec((depth, 8, tn), lambda j: (0, 0, j)),
        out_shape=jax.ShapeDtypeStruct((depth, 8, n), F32),
        compiler_params=_cparams(("arbitrary",)),
        name="ada_mod",
    )(c8, w_ada, b_ada.reshape(1, n), ada_table.reshape(depth, 1, n))


def _norm_mod(x, g, sc, sh):
    r = lax.rsqrt(jnp.mean(x * x, axis=-1, keepdims=True) + NORM_EPS)
    return ((x * r) * g) * (1.0 + sc) + sh


def _norm_kernel(x_ref, g_ref, sc_ref, sh_ref, o_ref):
    o_ref[...] = _norm_mod(x_ref[...], g_ref[...], sc_ref[...], sh_ref[...]).astype(o_ref.dtype)


def _norm_call(x, g, sc, sh, seq):
    n, d = x.shape
    tm = NORM_TM
    per_b = seq // tm
    return pl.pallas_call(
        _norm_kernel,
        grid=(n // tm,),
        in_specs=[
            pl.BlockSpec((tm, d), lambda i: (i, 0)),
            pl.BlockSpec((1, d), lambda i: (0, 0)),
            pl.BlockSpec((None, 1, d), lambda i: (i // per_b, 0, 0)),
            pl.BlockSpec((None, 1, d), lambda i: (i // per_b, 0, 0)),
        ],
        out_specs=pl.BlockSpec((tm, d), lambda i: (i, 0)),
        out_shape=jax.ShapeDtypeStruct((n, d), BF16),
        compiler_params=_cparams(("arbitrary",)),
        name="norm_mod",
    )(x, g.reshape(1, d), sc, sh)


def _rope_tables(seq):
    half = ROPE_DIM // 2
    inv_freq = jnp.power(ROPE_THETA, -jnp.arange(half, dtype=F32) * 2.0 / ROPE_DIM)
    ang = jnp.arange(seq, dtype=F32)[:, None] * inv_freq[None, :]
    cos, sin = jnp.cos(ang), jnp.sin(ang)
    ones = jnp.ones((seq, HEAD_DIM - ROPE_DIM), F32)
    c_k = jnp.concatenate([cos, cos, ones], axis=1)
    s_k = jnp.concatenate([-sin, sin, 0.0 * ones], axis=1)
    c_id = jnp.ones((seq, HEAD_DIM), F32)
    s_id = jnp.zeros((seq, HEAD_DIM), F32)
    return (jnp.stack([c_k * ATTN_SCALE, c_k, c_id]), jnp.stack([s_k * ATTN_SCALE, s_k, s_id]))


def _rope_apply(x, cos, sin):
    heads = x.shape[1] // HEAD_DIM
    lane = lax.broadcasted_iota(jnp.int32, x.shape, 1) & (HEAD_DIM - 1)
    half = ROPE_DIM // 2
    fwd = pltpu.roll(x, half, 1)
    bwd = pltpu.roll(x, x.shape[1] - half, 1)
    sw = jnp.where(lane < half, bwd, fwd)
    if heads > 1:
        cos = jnp.concatenate([cos] * heads, axis=1)
        sin = jnp.concatenate([sin] * heads, axis=1)
    return x * cos + sw * sin


def _dot_nt(a, wt):
    return lax.dot_general(a, wt.astype(BF16), (((1,), (1,)), ((), ())), preferred_element_type=F32)


EPI_CHUNKS = 4


def _proj_rope_kernel(a_ref, w_ref, cos_ref, sin_ref, o_ref):
    w = w_ref[...].astype(BF16)
    rc = a_ref.shape[0] // EPI_CHUNKS
    for r in range(EPI_CHUNKS):
        sl = slice(r * rc, (r + 1) * rc)
        acc = _dot_nt(a_ref[sl, :], w)
        o_ref[sl, :] = _rope_apply(acc, cos_ref[sl, :], sin_ref[sl, :]).astype(o_ref.dtype)


def _proj_kernel(a_ref, w_ref, o_ref):
    w = w_ref[0] if len(w_ref.shape) == 3 else w_ref[...]
    o_ref[...] = _dot_nt(a_ref[...], w).astype(o_ref.dtype)


def _proj_a(h, w_in, layer, cos_t, sin_t, seq):
    n, d = h.shape
    tm, tn = MM_TM, MM_TN
    per_b = seq // tm

    def variant(j):
        return jnp.where(j < 4, 0, jnp.where((j == 6) | (j == 8), 1, 2))

    return pl.pallas_call(
        _proj_rope_kernel,
        grid=(n // tm, SEG_A // tn),
        in_specs=[
            pl.BlockSpec((tm, d), lambda i, j: (i, 0)),
            pl.BlockSpec((None, tn, d), lambda i, j: (layer, j, 0)),
            pl.BlockSpec((None, tm, HEAD_DIM), lambda i, j: (variant(j), i % per_b, 0)),
            pl.BlockSpec((None, tm, HEAD_DIM), lambda i, j: (variant(j), i % per_b, 0)),
        ],
        out_specs=pl.BlockSpec((tm, tn), lambda i, j: (i, j)),
        out_shape=jax.ShapeDtypeStruct((n, SEG_A), BF16),
        compiler_params=_cparams(("arbitrary", "arbitrary")),
        name="proj_qkv",
    )(h, w_in, cos_t, sin_t)


def _proj_gate(h, w_in, layer):
    n, d = h.shape
    tm = MM_TM
    return pl.pallas_call(
        _proj_kernel,
        grid=(n // tm,),
        in_specs=[
            pl.BlockSpec((tm, d), lambda i: (i, 0)),
            pl.BlockSpec((None, LANES, d), lambda i: (layer, SEG_A // LANES, 0)),
        ],
        out_specs=pl.BlockSpec((tm, LANES), lambda i: (i, 0)),
        out_shape=jax.ShapeDtypeStruct((n, LANES), F32),
        compiler_params=_cparams(("arbitrary",)),
        name="proj_gate",
    )(h, w_in)


def _proj_r(h, w_in, layer):
    n, d = h.shape
    tm, tn = MM_TM, MM_TN
    return pl.pallas_call(
        _proj_kernel,
        grid=(n // tm, SEG_R // tn),
        in_specs=[
            pl.BlockSpec((tm, d), lambda i, j: (i, 0)),
            pl.BlockSpec((pl.Element(1), pl.Element(tn), pl.Element(d)), lambda i, j: (layer, pl.multiple_of(SEG_R0 + j * tn, 8), 0)),
        ],
        out_specs=pl.BlockSpec((tm, tn), lambda i, j: (i, j)),
        out_shape=jax.ShapeDtypeStruct((n, SEG_R), F32),
        compiler_params=_cparams(("arbitrary", "arbitrary")),
        name="proj_rnn_mix",
    )(h, w_in)


def _cmp_kernel(z_ref, w1_ref, w2_ref, pos_ref, cos_ref, sin_ref, o_ref):
    z = z_ref[...]
    w1 = w1_ref[...].astype(BF16)
    half = w1.shape[0] // 2
    a = _bdot(z, w1[:half])
    b = _bdot(z, w1[half:])
    pos = jnp.broadcast_to(pos_ref[...], (8, pos_ref.shape[1])).astype(BF16)
    bias = _bdot(pos, w1)[0:1]
    y = a + pltpu.roll(b, b.shape[0] - 1, 0) + bias
    hid = _gelu_tanh(y).astype(BF16)
    out = _bdot(hid, w2_ref[...].astype(BF16))
    groups = out.shape[0] // cos_ref.shape[0]
    cos = jnp.concatenate([cos_ref[...]] * groups, axis=0)
    sin = jnp.concatenate([sin_ref[...]] * groups, axis=0)
    o_ref[...] = _rope_apply(out, cos, sin).astype(o_ref.dtype)


def _compress(z2, w_cmp1, w_cmp2, cmp_pos, layer, cos_c, sin_c):
    _, b, rows, width = z2.shape
    n_grp = cos_c.shape[1]
    return pl.pallas_call(
        _cmp_kernel,
        grid=(2, b),
        in_specs=[
            pl.BlockSpec((None, None, rows, width), lambda j, bi: (j, bi, 0, 0)),
            pl.BlockSpec((None, None, 2 * width, CMP_HIDDEN), lambda j, bi: (layer, j, 0, 0)),
            pl.BlockSpec((None, None, CMP_HIDDEN, HEAD_DIM), lambda j, bi: (layer, j, 0, 0)),
            pl.BlockSpec((None, None, 1, 2 * width), lambda j, bi: (layer, j, 0, 0)),
            pl.BlockSpec((None, n_grp, HEAD_DIM), lambda j, bi: (j, 0, 0)),
            pl.BlockSpec((None, n_grp, HEAD_DIM), lambda j, bi: (j, 0, 0)),
        ],
        out_specs=pl.BlockSpec((None, None, rows, HEAD_DIM), lambda j, bi: (j, bi, 0, 0)),
        out_shape=jax.ShapeDtypeStruct((2, b, rows, HEAD_DIM), BF16),
        compiler_params=_cparams(("arbitrary", "arbitrary")),
        name="nsa_compress",
    )(z2, w_cmp1, w_cmp2, cmp_pos.reshape(cmp_pos.shape[0], 2, 1, CMP_BLOCK * HEAD_DIM), cos_c, sin_c)


def _attn_t_kernel(q_ref, kc_ref, vc_ref, ks_ref, vst_ref, kw_ref, vwt_ref, gl_ref, ovt_ref, ext_ref, o_ref):
    tq, tk = ATT_TQ, ATT_TK
    hpg = HEADS_PER_GROUP
    rows = hpg * tq
    i = pl.program_id(2)
    qb = q_ref[...]
    q4 = jnp.concatenate([qb[:, h * HEAD_DIM:(h + 1) * HEAD_DIM] for h in range(hpg)], axis=0)
    dn_nt = (((1,), (1,)), ((), ()))

    def tile_heads(x):
        return jnp.concatenate([x] * hpg, axis=1)

    n_cmp = kc_ref.shape[0] - 1
    key_n = lax.broadcasted_iota(jnp.int32, (LANES, tq), 0)
    t_pos = i * tq + lax.broadcasted_iota(jnp.int32, (LANES, tq), 1)
    cmask = (key_n < n_cmp) & (key_n * CMP_STRIDE + (CMP_BLOCK - 1) <= t_pos)
    cb4 = tile_heads(jnp.where(cmask, 0.0, NEG_INF))
    s = lax.dot_general(kc_ref[...], q4, dn_nt, preferred_element_type=F32) + cb4
    m = jnp.max(s, axis=0, keepdims=True)
    e = jnp.where(cb4 > -1.0, jnp.exp(s - m), 0.0)
    den = jnp.sum(e, axis=0, keepdims=True)
    p = e / jnp.maximum(den, 1e-30)
    vc_t = jnp.transpose(vc_ref[...].astype(F32)).astype(BF16)
    o_cmp = _bdot(vc_t, p.astype(BF16))
    p_sum = p[:, 0:tq]
    for h in range(1, hpg):
        p_sum = p_sum + p[:, h * tq:(h + 1) * tq]
    p_hi, p_lo = _split_bf16(p_sum)
    n_slc = ext_ref.shape[0] * (tk // SEL_BLOCK)
    imp = (_bdot(ovt_ref[...], p_hi) + _bdot(ovt_ref[...], p_lo))[:n_slc]

    blk = lax.broadcasted_iota(jnp.int32, (n_slc, tq), 0)
    cur = lax.shift_right_logical(i * tq + lax.broadcasted_iota(jnp.int32, (n_slc, tq), 1), int(math.log2(SEL_BLOCK)))
    visible = blk <= cur
    forced = (blk == 0) | (visible & (blk > cur - N_LOCAL_SEL))
    score = jnp.where(visible, jnp.where(forced, FORCED_SCORE, imp), NEG_INF)
    rank = jnp.zeros((n_slc, tq), F32)
    for k in range(n_slc):
        row = score[k:k + 1, :]
        beats = (row > score) | ((row == score) & (blk > k))
        rank = rank + jnp.where(beats, 1.0, 0.0)
    sel = jnp.where((rank < float(min(N_SELECT, n_slc))) & visible, 1.0, 0.0)
    sel = jnp.concatenate([sel, jnp.zeros((LANES - n_slc, tq), F32)], axis=0).astype(BF16)

    rel = (lax.broadcasted_iota(jnp.int32, (tk, tq), 1) + i * tq) - lax.broadcasted_iota(jnp.int32, (tk, tq), 0)

    def flash_step(c, carry, k_ref, vt_ref, bias):
        m_p, l_p, acc_p = carry
        off = pl.multiple_of(c * tk, tk)
        kk = k_ref[pl.ds(off, tk), :]
        sc = lax.dot_general(kk, q4, dn_nt, preferred_element_type=F32) + tile_heads(bias)
        m_n = jnp.maximum(m_p, jnp.max(sc, axis=0, keepdims=True))
        alpha = jnp.exp(m_p - m_n)
        pp = jnp.exp(sc - m_n)
        l_n = alpha * l_p + jnp.sum(pp, axis=0, keepdims=True)
        acc_n = alpha * acc_p + _bdot(vt_ref[c], pp.astype(BF16))
        return m_n, l_n, acc_n

    init = (jnp.full((1, rows), NEG_INF, F32), jnp.zeros((1, rows), F32), jnp.zeros((HEAD_DIM, rows), F32))

    def sel_body(c, carry):
        kmask = _bdot(ext_ref[c], sel) > 0.5
        ok = kmask & (rel - c * tk >= 0)
        return flash_step(c, carry, ks_ref, vst_ref, jnp.where(ok, 0.0, NEG_INF))

    _, l_s, acc_s = lax.fori_loop(0, i + 1, sel_body, init)

    def win_body(c, carry):
        d = rel - c * tk
        ok = (d >= 0) & (d < WINDOW)
        return flash_step(c, carry, kw_ref, vwt_ref, jnp.where(ok, 0.0, NEG_INF))

    c0 = jnp.maximum(i - WINDOW // tk, 0)
    _, l_w, acc_w = lax.fori_loop(c0, i + 1, win_body, init)

    o_sel = acc_s / l_s
    o_win = acc_w / l_w
    gates = _sigmoid(gl_ref[...])
    outs = []
    for h in range(hpg):
        sl = slice(h * tq, (h + 1) * tq)
        o_h = (gates[3 * h:3 * h + 1] * o_cmp[:, sl] + gates[3 * h + 1:3 * h + 2] * o_sel[:, sl]
               + gates[3 * h + 2:3 * h + 3] * o_win[:, sl])
        outs.append(jnp.transpose(o_h))
    o_ref[...] = jnp.concatenate(outs, axis=1).astype(o_ref.dtype)


def _attention_t(za, kvc, vst, vwt, gl_t, ov_t, ex_t, batch, seq):
    n = za.shape[0]
    tq = ATT_TQ
    per_b = seq // tq
    g = N_KV_HEADS
    n_cb = kvc.shape[2] // g
    kv_blk = lambda col0: pl.BlockSpec((seq, HEAD_DIM), lambda b, gi, i: (b, col0 + gi))
    vt_blk = pl.BlockSpec((None, None) + vst.shape[2:], lambda b, gi, i: (b, gi, 0, 0, 0))
    q_cols = Q_WIDTH // HEAD_DIM
    return pl.pallas_call(
        _attn_t_kernel,
        grid=(batch, g, per_b),
        in_specs=[
            pl.BlockSpec((tq, HEADS_PER_GROUP * HEAD_DIM), lambda b, gi, i: (b * per_b + i, gi)),
            pl.BlockSpec((None, None, n_cb, HEAD_DIM), lambda b, gi, i: (0, b, gi, 0)),
            pl.BlockSpec((None, None, n_cb, HEAD_DIM), lambda b, gi, i: (1, b, gi, 0)),
            kv_blk(q_cols + 2 * g), vt_blk, kv_blk(q_cols + 4 * g), vt_blk,
            pl.BlockSpec((None, gl_t.shape[1], tq), lambda b, gi, i: (gi, 0, b * per_b + i)),
            pl.BlockSpec((LANES, LANES), lambda b, gi, i: (0, 0)),
            pl.BlockSpec(ex_t.shape, lambda b, gi, i: (0, 0, 0)),
        ],
        out_specs=pl.BlockSpec((tq, HEADS_PER_GROUP * HEAD_DIM), lambda b, gi, i: (b * per_b + i, gi)),
        out_shape=jax.ShapeDtypeStruct((n, Q_WIDTH), BF16),
        compiler_params=_cparams(("arbitrary", "arbitrary", "arbitrary")),
        name="nsa_attention",
    )(za, kvc, kvc, za, vst, za, vwt, gl_t, ov_t, ex_t)


def _rglru_kernel(x_ref, g_ref, p_ref, wg_ref, o_ref, xbuf, hc):
    tt, cb = x_ref.shape
    step = pl.program_id(2)

    @pl.when(step == 0)
    def _():
        xbuf[0:8, :] = jnp.zeros((8, cb), F32)
        hc[...] = jnp.zeros_like(hc)

    x = x_ref[...]
    xbuf[8:8 + tt, :] = x
    prm = p_ref[...]
    xc = prm[4:5] + prm[3:4] * x
    for k in range(CONV_WIDTH - 1):
        shift = CONV_WIDTH - 1 - k
        xc = xc + prm[k:k + 1] * xbuf[8 - shift:8 - shift + tt, :]
    xbuf[0:8, :] = x[tt - 8:tt]

    xcb = xc.astype(BF16)
    pre_r, pre_i = [], []
    for h in range(cb // HEAD_DIM):
        xh = xcb[:, h * HEAD_DIM:(h + 1) * HEAD_DIM]
        pre_r.append(_bdot(xh, wg_ref[0, h].astype(BF16)))
        pre_i.append(_bdot(xh, wg_ref[1, h].astype(BF16)))
    r = _sigmoid(jnp.concatenate(pre_r, axis=1) + prm[5:6])
    ig = _sigmoid(jnp.concatenate(pre_i, axis=1) + prm[6:7])
    zl = -prm[7:8]
    softplus = jnp.maximum(zl, 0.0) + jnp.log1p(jnp.exp(-jnp.abs(zl)))
    log_a = (-LRU_C) * r * softplus
    a = jnp.exp(log_a)
    b = jnp.sqrt(1.0 - a * a) * (ig * xc)

    row = lax.broadcasted_iota(jnp.int32, (tt, cb), 0)
    s = 1
    while s < tt:
        keep = row >= s
        a_sh = jnp.where(keep, pltpu.roll(a, s, 0), 1.0)
        b_sh = jnp.where(keep, pltpu.roll(b, s, 0), 0.0)
        b = a * b_sh + b
        a = a * a_sh
        s *= 2
    hcur = b + a * hc[0:1, :]
    hc[...] = jnp.broadcast_to(hcur[tt - 1:tt], hc.shape)
    o_ref[...] = (hcur * _gelu_tanh(g_ref[...])).astype(o_ref.dtype)


def _rglru(zr, prm, w_gate, batch, seq):
    n = zr.shape[0]
    tt, cb = RNN_TT, RNN_CB
    per_b = seq // tt
    n_cb = RNN_WIDTH // cb
    hpb = cb // HEAD_DIM
    return pl.pallas_call(
        _rglru_kernel,
        grid=(batch, n_cb, per_b),
        in_specs=[
            pl.BlockSpec((tt, cb), lambda b, c, t: (b * per_b + t, c)),
            pl.BlockSpec((tt, cb), lambda b, c, t: (b * per_b + t, n_cb + c)),
            pl.BlockSpec((8, cb), lambda b, c, t: (0, c)),
            pl.BlockSpec((2, hpb, HEAD_DIM, HEAD_DIM), lambda b, c, t: (0, c, 0, 0)),
        ],
        out_specs=pl.BlockSpec((tt, cb), lambda b, c, t: (b * per_b + t, c)),
        out_shape=jax.ShapeDtypeStruct((n, RNN_WIDTH), BF16),
        scratch_shapes=[pltpu.VMEM((tt + 8, cb), F32), pltpu.VMEM((8, cb), F32)],
        compiler_params=_cparams(("arbitrary", "arbitrary", "arbitrary")),
        name="rglru",
    )(zr, zr, prm, w_gate)


def _mix_kernel(a1_ref, a2_ref, w1_ref, w2_ref, m1_ref, m2_ref, o_ref):
    w1 = w1_ref[...].astype(BF16)
    w2 = w2_ref[...].astype(BF16)
    rc = a1_ref.shape[0] // EPI_CHUNKS
    for r in range(EPI_CHUNKS):
        sl = slice(r * rc, (r + 1) * rc)
        y1 = _bdot(a1_ref[sl, :], w1)
        y2 = _bdot(a2_ref[sl, :], w2)
        o_ref[sl, :] = (_sigmoid(m1_ref[sl, :]) * y1 + _sigmoid(m2_ref[sl, :]) * y2).astype(o_ref.dtype)


def _mix(o_attn, o_rnn, w_branch, layer, zr):
    n, k = o_attn.shape
    tm, tn = MM_TM, MM_TN
    m_off = 2 * RNN_WIDTH // tn
    return pl.pallas_call(
        _mix_kernel,
        grid=(n // tm, D_MODEL // tn),
        in_specs=[
            pl.BlockSpec((tm, k), lambda i, j: (i, 0)),
            pl.BlockSpec((tm, k), lambda i, j: (i, 0)),
            pl.BlockSpec((None, None, k, tn), lambda i, j: (layer, 0, 0, j)),
            pl.BlockSpec((None, None, k, tn), lambda i, j: (layer, 1, 0, j)),
            pl.BlockSpec((tm, tn), lambda i, j: (i, m_off + j)),
            pl.BlockSpec((tm, tn), lambda i, j: (i, m_off + D_MODEL // tn + j)),
        ],
        out_specs=pl.BlockSpec((tm, tn), lambda i, j: (i, j)),
        out_shape=jax.ShapeDtypeStruct((n, D_MODEL), BF16),
        compiler_params=_cparams(("arbitrary", "arbitrary")),
        name="mixer_merge",
    )(o_attn, o_rnn, w_branch, w_branch, zr, zr)


def _out_kernel(a_ref, w_ref, x_ref, g_ref, o_ref):
    o_ref[...] = x_ref[...] + g_ref[...] * _bdot(a_ref[...], w_ref[...].astype(BF16))


def _out_proj(mixed, w_out, layer, x, gate, seq):
    n, d = x.shape
    tm, tn = MM_TM, MM_TN
    per_b = seq // tm
    return pl.pallas_call(
        _out_kernel,
        grid=(n // tm, d // tn),
        in_specs=[
            pl.BlockSpec((tm, d), lambda i, j: (i, 0)),
            pl.BlockSpec((None, d, tn), lambda i, j: (layer, 0, j)),
            pl.BlockSpec((tm, tn), lambda i, j: (i, j)),
            pl.BlockSpec((None, 1, tn), lambda i, j: (i // per_b, 0, j)),
        ],
        out_specs=pl.BlockSpec((tm, tn), lambda i, j: (i, j)),
        out_shape=jax.ShapeDtypeStruct((n, d), F32),
        compiler_params=_cparams(("arbitrary", "arbitrary")),
        name="out_proj",
    )(mixed, w_out, x, gate)


def _route_kernel(x_ref, g_ref, sc_ref, sh_ref, wr_ref, br_ref, h_ref, meta_ref, cnt_ref, run):
    tm = x_ref.shape[0]

    @pl.when(pl.program_id(0) == 0)
    def _():
        run[...] = jnp.zeros_like(run)

    h = _norm_mod(x_ref[...], g_ref[...], sc_ref[...], sh_ref[...])
    h_ref[...] = h
    h_hi, h_lo = _split_bf16(h)
    w_hi, w_lo = _split_bf16(wr_ref[...])
    logits = _bdot(h_hi, w_hi) + _bdot(h_hi, w_lo) + _bdot(h_lo, w_hi) + br_ref[...]
    lane = lax.broadcasted_iota(jnp.int32, (tm, LANES), 1)
    lane_f = lane.astype(F32)
    cur = jnp.where(lane < N_EXPERTS, logits, -jnp.inf)
    vals, idxs = [], []
    for _ in range(TOP_K):
        mx = jnp.max(cur, axis=-1, keepdims=True)
        ix = jnp.min(jnp.where(cur == mx, lane_f, float(LANES)), axis=-1, keepdims=True)
        vals.append(mx)
        idxs.append(ix)
        cur = jnp.where(lane_f == ix, -jnp.inf, cur)
    exps = [jnp.exp(v - vals[0]) for v in vals]
    den = exps[0]
    for e in exps[1:]:
        den = den + e
    hot = [lane_f == ix for ix in idxs]
    cnt = jnp.zeros((tm, LANES), F32)
    for hk in hot:
        cnt = cnt + jnp.where(hk, 1.0, 0.0)
    tri = jnp.where(lax.broadcasted_iota(jnp.int32, (tm, tm), 0) > lax.broadcasted_iota(jnp.int32, (tm, tm), 1),
                    1.0, 0.0).astype(BF16)
    pos = _bdot(tri, cnt.astype(BF16)) + run[0:1, :]
    meta = jnp.zeros((tm, LANES), F32)
    for k in range(TOP_K):
        pk = jnp.sum(jnp.where(hot[k], pos, 0.0), axis=-1, keepdims=True)
        meta = jnp.where(lane == k, idxs[k], meta)
        meta = jnp.where(lane == TOP_K + k, pk, meta)
        meta = jnp.where(lane == 2 * TOP_K + k, exps[k] / den, meta)
    meta_ref[...] = meta
    run[...] = run[...] + jnp.sum(cnt, axis=0, keepdims=True)
    cnt_ref[...] = run[...]


def _route(x, g, sc, sh, w_router, b_router, seq):
    n, d = x.shape
    tm = NORM_TM
    per_b = seq // tm
    wr = jnp.zeros((d, LANES), F32).at[:, :N_EXPERTS].set(w_router)
    br = jnp.zeros((1, LANES), F32).at[0, :N_EXPERTS].set(b_router)
    return pl.pallas_call(
        _route_kernel,
        grid=(n // tm,),
        in_specs=[
            pl.BlockSpec((tm, d), lambda i: (i, 0)),
            pl.BlockSpec((1, d), lambda i: (0, 0)),
            pl.BlockSpec((None, 1, d), lambda i: (i // per_b, 0, 0)),
            pl.BlockSpec((None, 1, d), lambda i: (i // per_b, 0, 0)),
            pl.BlockSpec((d, LANES), lambda i: (0, 0)),
            pl.BlockSpec((1, LANES), lambda i: (0, 0)),
        ],
        out_specs=[
            pl.BlockSpec((tm, d), lambda i: (i, 0)),
            pl.BlockSpec((tm, LANES), lambda i: (i, 0)),
            pl.BlockSpec((8, LANES), lambda i: (0, 0)),
        ],
        out_shape=[
            jax.ShapeDtypeStruct((n, d), F32),
            jax.ShapeDtypeStruct((n, LANES), F32),
            jax.ShapeDtypeStruct((8, LANES), F32),
        ],
        scratch_shapes=[pltpu.VMEM((8, LANES), F32)],
        compiler_params=_cparams(("arbitrary",)),
        name="moe_route",
    )(x, g.reshape(1, d), sc, sh, wr, br)


def _expert_kernel(te_ref, nt_ref, tok_ref, h_hbm, w1_ref, b1_ref, w2_ref, b2_ref, even_ref, o_ref, xbuf, sem):
    tm = o_ref.shape[0]
    j = pl.program_id(0)
    nt = nt_ref[0]
    nbuf = xbuf.shape[0]
    ahead = nbuf - 1
    cur = j % nbuf

    def start_gather(tile, buf):
        base = tile * tm

        def body(ro, c):
            r0 = pl.multiple_of(ro * 8, 8)
            for ri in range(8):
                tok = tok_ref[base + r0 + ri]
                pltpu.make_async_copy(h_hbm.at[pl.ds(tok, 1), :], xbuf.at[buf, pl.ds(r0 + ri, 1), :],
                                      sem.at[buf]).start()
            return c
        lax.fori_loop(0, tm // 8, body, 0)

    def wait_gather(buf):
        pltpu.make_async_copy(h_hbm.at[pl.ds(0, tm), :], xbuf.at[buf], sem.at[buf]).wait()

    @pl.when(j == 0)
    def _():
        for t in range(ahead):
            start_gather(jnp.minimum(t, nt - 1), t)

    @pl.when(j < nt)
    def _():
        wait_gather(cur)
        x = xbuf[cur].astype(BF16)
        z = _bdot(x, w1_ref[...]) + b1_ref[...]
        z_next = pltpu.roll(z, z.shape[1] - 1, 1)
        glu = jnp.minimum(z, SWIGLU_LIMIT)
        lin = jnp.clip(z_next, -SWIGLU_LIMIT, SWIGLU_LIMIT)
        act = (glu * _sigmoid(SWIGLU_ALPHA * glu) * (lin + 1.0)).astype(BF16)
        act = _bdot(act, even_ref[...]).astype(BF16)
        o_ref[...] = _bdot(act, w2_ref[...]) + b2_ref[...]
        nxt = jnp.minimum(j + ahead, nt - 1) * tm
        nbf = (j + ahead) % nbuf
        for r in range(tm):
            tok = tok_ref[nxt + r]
            pltpu.make_async_copy(h_hbm.at[pl.ds(tok, 1), :], xbuf.at[nbf, pl.ds(r, 1), :], sem.at[nbf]).start()

    @pl.when((j >= nt) & (j < nt + ahead))
    def _():
        wait_gather(cur)

    @pl.when(j >= nt)
    def _():
        o_ref[...] = jnp.zeros_like(o_ref)


def _experts(h2, tile_expert, n_tiles, tok, w1, b1, w2, b2, layer):
    n, d = h2.shape
    tm = MOE_TM
    max_tiles = tile_expert.shape[0]
    ff2 = w1.shape[3]
    ff = ff2 // 2
    even = np.zeros((ff2, ff), np.float32)
    even[2 * np.arange(ff), np.arange(ff)] = 1.0
    e_map = lambda j, te, nt, tk: (layer, te[j], 0, 0)
    return pl.pallas_call(
        _expert_kernel,
        grid_spec=pltpu.PrefetchScalarGridSpec(
            num_scalar_prefetch=3,
            grid=(max_tiles,),
            in_specs=[
                pl.BlockSpec(memory_space=pl.ANY),
                pl.BlockSpec((None, None, d, ff2), e_map),
                pl.BlockSpec((None, None, 1, ff2), e_map),
                pl.BlockSpec((None, None, ff, d), e_map),
                pl.BlockSpec((None, None, 1, d), e_map),
                pl.BlockSpec((ff2, ff), lambda j, te, nt, tk: (0, 0)),
            ],
            out_specs=pl.BlockSpec((tm, d), lambda j, te, nt, tk: (j, 0)),
            scratch_shapes=[pltpu.VMEM((MOE_GATHER_BUFS, tm, d), F32), pltpu.SemaphoreType.DMA((MOE_GATHER_BUFS,))],
        ),
        out_shape=jax.ShapeDtypeStruct((max_tiles * tm, d), F32),
        compiler_params=_cparams(("arbitrary",)),
        name="moe_experts",
    )(tile_expert, n_tiles, tok, h2, w1, b1, w2, b2, jnp.asarray(even, BF16))


def _combine_kernel(slot_ref, y_hbm, x_ref, w_ref, g_ref, gf_ref, o_ref, ybuf, sem, *, final_norm):
    tt = x_ref.shape[0]
    i = pl.program_id(0)
    nsteps = pl.num_programs(0)
    cur = i % 2

    def start_gather(step, buf):
        base = step * (tt * TOP_K)

        def body(ro, c):
            r0 = pl.multiple_of(ro * 8, 8)
            for ri in range(8):
                for k in range(TOP_K):
                    s = slot_ref[base + (r0 + ri) * TOP_K + k]
                    pltpu.make_async_copy(y_hbm.at[pl.ds(s, 1), :], ybuf.at[buf, k, pl.ds(r0 + ri, 1), :],
                                          sem.at[buf]).start()
            return c
        lax.fori_loop(0, tt // 8, body, 0)

    @pl.when(i == 0)
    def _():
        start_gather(0, 0)

    @pl.when(i + 1 < nsteps)
    def _():
        start_gather(i + 1, 1 - cur)

    for k in range(TOP_K):
        pltpu.make_async_copy(y_hbm.at[pl.ds(0, tt), :], ybuf.at[cur, k], sem.at[cur]).wait()
    w = w_ref[...]
    y = w[:, 0:1] * ybuf[cur, 0]
    for k in range(1, TOP_K):
        y = y + w[:, k:k + 1] * ybuf[cur, k]
    out = x_ref[...] + g_ref[...] * y
    if final_norm:
        r = lax.rsqrt(jnp.mean(out * out, axis=-1, keepdims=True) + NORM_EPS)
        out = (out * r) * gf_ref[...]
    o_ref[...] = out


def _combine(slots, y_slots, x, wts, gate, g_final, seq, final_norm):
    n, d = x.shape
    tt = COMB_TT
    per_b = seq // tt
    return pl.pallas_call(
        functools.partial(_combine_kernel, final_norm=final_norm),
        grid_spec=pltpu.PrefetchScalarGridSpec(
            num_scalar_prefetch=1,
            grid=(n // tt,),
            in_specs=[
                pl.BlockSpec(memory_space=pl.ANY),
                pl.BlockSpec((tt, d), lambda i, s: (i, 0)),
                pl.BlockSpec((tt, TOP_K), lambda i, s: (i, 0)),
                pl.BlockSpec((None, 1, d), lambda i, s: (i // per_b, 0, 0)),
                pl.BlockSpec((1, d), lambda i, s: (0, 0)),
            ],
            out_specs=pl.BlockSpec((tt, d), lambda i, s: (i, 0)),
            scratch_shapes=[pltpu.VMEM((2, TOP_K, tt, d), F32), pltpu.SemaphoreType.DMA((2,))],
        ),
        out_shape=jax.ShapeDtypeStruct((n, d), F32),
        compiler_params=_cparams(("arbitrary",)),
        name="moe_combine",
    )(slots, y_slots, x, wts, gate, g_final.reshape(1, d))


def _moe(x, g2, sc, sh, gate, w_router, b_router, w1, b1, w2, b2, layer, g_final, seq, final_norm):
    n, d = x.shape
    tm = MOE_TM
    max_tiles = (n * TOP_K + N_EXPERTS * (tm - 1)) // tm + MOE_GATHER_BUFS - 1
    h2, meta, cnt = _route(x, g2, sc, sh, w_router, b_router, seq)

    idx = meta[:, 0:TOP_K].astype(jnp.int32)
    pos = meta[:, TOP_K:2 * TOP_K].astype(jnp.int32)
    wts = meta[:, 2 * TOP_K:3 * TOP_K]
    counts = cnt[0, :N_EXPERTS].astype(jnp.int32)
    tiles_e = (counts + tm - 1) // tm
    tile_end = jnp.cumsum(tiles_e)
    offsets = (tile_end - tiles_e) * tm
    n_tiles = tile_end[-1:]
    tile_ids = jnp.arange(max_tiles, dtype=jnp.int32)
    tile_expert = jnp.minimum(jnp.sum((tile_end[None, :] <= tile_ids[:, None]).astype(jnp.int32), axis=1),
                              N_EXPERTS - 1)
    slot = offsets[idx] + pos
    tok_ids = jnp.broadcast_to(jnp.arange(n, dtype=jnp.int32)[:, None], slot.shape)
    tok = jnp.zeros((max_tiles * tm,), jnp.int32).at[slot.reshape(-1)].set(tok_ids.reshape(-1))

    y_slots = _experts(h2, tile_expert, n_tiles, tok, w1, b1, w2, b2, layer)
    return _combine(slot.reshape(-1), y_slots, x, wts, gate, g_final, seq, final_norm)


def _overlap_matrix(seq):
    n_cmp = (seq - CMP_BLOCK) // CMP_STRIDE + 1
    n_slc = seq // SEL_BLOCK
    cmp_start = np.arange(n_cmp) * CMP_STRIDE
    slc_start = np.arange(n_slc) * SEL_BLOCK
    ov = np.clip(np.minimum(cmp_start[:, None] + CMP_BLOCK, slc_start[None, :] + SEL_BLOCK)
                 - np.maximum(cmp_start[:, None], slc_start[None, :]), 0, None) / CMP_BLOCK
    out = np.zeros((LANES, LANES), np.float32)
    out[:n_slc, :n_cmp] = ov.T
    return jnp.asarray(out, BF16)


def _expand_matrix(seq):
    n_chunks = seq // ATT_TK
    key = np.arange(seq).reshape(n_chunks, ATT_TK, 1)
    blk = np.arange(LANES).reshape(1, 1, LANES)
    return jnp.asarray((key // SEL_BLOCK == blk).astype(np.float32), BF16)


def _cmp_rope_tables(seq):
    n_cmp = (seq - CMP_BLOCK) // CMP_STRIDE + 1
    half = ROPE_DIM // 2
    inv_freq = jnp.power(ROPE_THETA, -jnp.arange(half, dtype=F32) * 2.0 / ROPE_DIM)
    pos = jnp.arange(n_cmp + 1, dtype=F32) * CMP_STRIDE + (CMP_BLOCK - 1)
    ang = pos[:, None] * inv_freq[None, :]
    cos, sin = jnp.cos(ang), jnp.sin(ang)
    ones = jnp.ones((n_cmp + 1, HEAD_DIM - ROPE_DIM), F32)
    c_k = jnp.concatenate([cos, cos, ones], axis=1)
    s_k = jnp.concatenate([-sin, sin, 0.0 * ones], axis=1)
    return jnp.stack([c_k, jnp.ones_like(c_k)]), jnp.stack([s_k, jnp.zeros_like(s_k)])


def kernel(x, c, w_ada, b_ada, ada_table, g_norm1, w_in, w_cmp1, w_cmp2, cmp_pos, conv_w, conv_b,
           w_lru_gate, b_lru_gate, lru_lambda, w_branch, w_out, g_norm2, w_router, b_router,
           w_exp1, b_exp1, w_exp2, b_exp2, g_final):
    batch, seq, d = x.shape
    depth = w_in.shape[0]
    n = batch * seq
    g = N_KV_HEADS
    assert d == D_MODEL and seq % MM_TM == 0 and seq % ATT_TQ == 0 and n % COMB_TT == 0

    mod = _ada_mod(c, w_ada, b_ada, ada_table)
    cos_t, sin_t = _rope_tables(seq)
    cos_c, sin_c = _cmp_rope_tables(seq)
    ov_t = _overlap_matrix(seq)
    ex_t = _expand_matrix(seq)
    w_in = jnp.swapaxes(w_in, 1, 2)
    w1_bf = w_exp1.astype(BF16)
    w2_bf = w_exp2.astype(BF16)
    b1_e = b_exp1[:, :, None, :]
    b2_e = b_exp2[:, :, None, :]

    xf = x.reshape(n, d)
    for l in range(depth):
        m = [mod[l, :batch, j * d:(j + 1) * d].reshape(batch, 1, d) for j in range(N_ADA)]
        shift1, scale1, gate1, shift2, scale2, gate2 = m

        h = _norm_call(xf, g_norm1[l], scale1, shift1, seq)
        za = _proj_a(h, w_in, l, cos_t, sin_t, seq)
        zg = _proj_gate(h, w_in, l)
        zr = _proj_r(h, w_in, l)

        kv = za[:, Q_WIDTH:Q_WIDTH + 2 * KV_WIDTH].reshape(batch, seq // CMP_STRIDE, CMP_STRIDE, 2, g, HEAD_DIM)
        z2 = kv.transpose(3, 0, 4, 1, 2, 5).reshape(2, batch, g * (seq // CMP_STRIDE), CMP_STRIDE * HEAD_DIM)
        kvc = _compress(z2, w_cmp1, w_cmp2, cmp_pos, l, cos_c, sin_c)
        gl_t = zg[:, :GATE_W].reshape(n, g, 3 * HEADS_PER_GROUP).transpose(1, 2, 0)
        gl_t = jnp.pad(gl_t, ((0, 0), (0, 16 - 3 * HEADS_PER_GROUP), (0, 0)))

        def values_t(col0):
            v = za[:, col0:col0 + KV_WIDTH].reshape(batch, seq // ATT_TK, ATT_TK, g, HEAD_DIM)
            return v.transpose(0, 3, 1, 4, 2)

        vst = values_t(Q_WIDTH + 3 * KV_WIDTH)
        vwt = values_t(Q_WIDTH + 5 * KV_WIDTH)
        o_attn = _attention_t(za, kvc, vst, vwt, gl_t, ov_t, ex_t, batch, seq)

        prm = jnp.concatenate([conv_w[l], conv_b[l][None], b_lru_gate[l], lru_lambda[l][None]], axis=0)
        o_rnn = _rglru(zr, prm, w_lru_gate[l], batch, seq)

        mixed = _mix(o_attn, o_rnn, w_branch, l, zr)
        xf = _out_proj(mixed, w_out, l, xf, gate1, seq)

        xf = _moe(xf, g_norm2[l], scale2, shift2, gate2, w_router[l], b_router[l], w1_bf, b1_e, w2_bf, b2_e, l,
                  g_final, seq, final_norm=(l == depth - 1))
    return xf.reshape(batch, seq, d)
```

```python
import functools
import math

import numpy as np
import jax
import jax.numpy as jnp
from jax import lax
from jax.experimental import pallas as pl
from jax.experimental.pallas import tpu as pltpu

F32 = jnp.float32
BF16 = jnp.bfloat16

D_MODEL = 4096
HEAD_DIM = 128
N_Q_HEADS = 16
N_KV_HEADS = 4
HEADS_PER_GROUP = 4
Q_WIDTH = N_Q_HEADS * HEAD_DIM
KV_WIDTH = N_KV_HEADS * HEAD_DIM
ATTN_SCALE = HEAD_DIM ** -0.5
ROPE_THETA = 500000.0
ROPE_DIM = 32
CMP_BLOCK = 32
CMP_STRIDE = 16
CMP_HIDDEN = 256
SEL_BLOCK = 64
N_SELECT = 16
N_LOCAL_SEL = 2
WINDOW = 512
RNN_WIDTH = 2048
RNN_HEADS = 16
CONV_WIDTH = 4
LRU_C = 8.0
N_EXPERTS = 32
TOP_K = 4
EXPERT_FF = 512
SWIGLU_LIMIT = 7.0
SWIGLU_ALPHA = 1.702
N_ADA = 6
NORM_EPS = 1e-6
NEG_INF = -1e30
FORCED_SCORE = 1e6

SEG_A = Q_WIDTH + 6 * KV_WIDTH
GATE_W = 3 * N_Q_HEADS
SEG_R0 = SEG_A + GATE_W
SEG_R = 2 * RNN_WIDTH + 2 * D_MODEL

LANES = 128
VMEM_LIMIT_MB = 56
MM_TM = 1024
MM_TN = 512
ATT_TQ = 256
ATT_TK = 256
RNN_TT = 256
RNN_CB = 256
NORM_TM = 256
MOE_TM = 256
MOE_GATHER_BUFS = 3
MOE_STAGE_BYTES = 2 << 20
COMB_TT = 128


def _cparams(sem):
    return pltpu.CompilerParams(dimension_semantics=sem, vmem_limit_bytes=VMEM_LIMIT_MB << 20)


def _sigmoid(x):
    return 1.0 / (1.0 + jnp.exp(-x))


def _gelu_tanh(x):
    return 0.5 * x * (1.0 + jnp.tanh(0.7978845608028654 * (x + 0.044715 * (x * x * x))))


def _bdot(a, b):
    return jnp.dot(a, b, preferred_element_type=F32)


def _split_bf16(x):
    hi = x.astype(BF16)
    lo = (x - hi.astype(F32)).astype(BF16)
    return hi, lo


def _ada_kernel(c_ref, w_ref, b_ref, t_ref, o_ref):
    c = c_ref[...]
    s = (c * _sigmoid(c)).astype(BF16)
    emb = _bdot(s, w_ref[...].astype(BF16)) + b_ref[...]
    for l in range(o_ref.shape[0]):
        o_ref[l] = emb + t_ref[l]


def _ada_mod(c, w_ada, b_ada, ada_table):
    depth = ada_table.shape[0]
    b = c.shape[0]
    n = w_ada.shape[1]
    c8 = jnp.zeros((8, D_MODEL), F32).at[:b].set(c)
    tn = 512
    return pl.pallas_call(
        _ada_kernel,
        grid=(n // tn,),
        in_specs=[
            pl.BlockSpec((8, D_MODEL), lambda j: (0, 0)),
            pl.BlockSpec((D_MODEL, tn), lambda j: (0, j)),
            pl.BlockSpec((1, tn), lambda j: (0, j)),
            pl.BlockSpec((depth, 1, tn), lambda j: (0, 0, j)),
        ],
        out_specs=pl.BlockSpec((depth, 8, tn), lambda j: (0, 0, j)),
        out_shape=jax.ShapeDtypeStruct((depth, 8, n), F32),
        compiler_params=_cparams(("arbitrary",)),
        name="ada_mod",
    )(c8, w_ada, b_ada.reshape(1, n), ada_table.reshape(depth, 1, n))


def _norm_mod(x, g, sc, sh):
    r = lax.rsqrt(jnp.mean(x * x, axis=-1, keepdims=True) + NORM_EPS)
    return ((x * r) * g) * (1.0 + sc) + sh


def _norm_kernel(x_ref, g_ref, sc_ref, sh_ref, o_ref):
    o_ref[...] = _norm_mod(x_ref[...], g_ref[...], sc_ref[...], sh_ref[...]).astype(o_ref.dtype)


def _norm_call(x, g, sc, sh, seq):
    n, d = x.shape
    tm = NORM_TM
    per_b = seq // tm
    return pl.pallas_call(
        _norm_kernel,
        grid=(n // tm,),
        in_specs=[
            pl.BlockSpec((tm, d), lambda i: (i, 0)),
            pl.BlockSpec((1, d), lambda i: (0, 0)),
            pl.BlockSpec((None, 1, d), lambda i: (i // per_b, 0, 0)),
            pl.BlockSpec((None, 1, d), lambda i: (i // per_b, 0, 0)),
        ],
        out_specs=pl.BlockSpec((tm, d), lambda i: (i, 0)),
        out_shape=jax.ShapeDtypeStruct((n, d), BF16),
        compiler_params=_cparams(("arbitrary",)),
        name="norm_mod",
    )(x, g.reshape(1, d), sc, sh)


def _rope_tables(seq):
    half = ROPE_DIM // 2
    inv_freq = jnp.power(ROPE_THETA, -jnp.arange(half, dtype=F32) * 2.0 / ROPE_DIM)
    ang = jnp.arange(seq, dtype=F32)[:, None] * inv_freq[None, :]
    cos, sin = jnp.cos(ang), jnp.sin(ang)
    ones = jnp.ones((seq, HEAD_DIM - ROPE_DIM), F32)
    c_k = jnp.concatenate([cos, cos, ones], axis=1)
    s_k = jnp.concatenate([-sin, sin, 0.0 * ones], axis=1)
    c_id = jnp.ones((seq, HEAD_DIM), F32)
    s_id = jnp.zeros((seq, HEAD_DIM), F32)
    return (jnp.stack([c_k * ATTN_SCALE, c_k, c_id]), jnp.stack([s_k * ATTN_SCALE, s_k, s_id]))


def _rope_apply(x, cos, sin):
    heads = x.shape[1] // HEAD_DIM
    lane = lax.broadcasted_iota(jnp.int32, x.shape, 1) & (HEAD_DIM - 1)
    half = ROPE_DIM // 2
    fwd = pltpu.roll(x, half, 1)
    bwd = pltpu.roll(x, x.shape[1] - half, 1)
    sw = jnp.where(lane < half, bwd, fwd)
    if heads > 1:
        cos = jnp.concatenate([cos] * heads, axis=1)
        sin = jnp.concatenate([sin] * heads, axis=1)
    return x * cos + sw * sin


def _dot_nt(a, wt):
    return lax.dot_general(a, wt.astype(BF16), (((1,), (1,)), ((), ())), preferred_element_type=F32)


EPI_CHUNKS = 4


def _proj_rope_kernel(a_ref, w_ref, cos_ref, sin_ref, o_ref):
    w = w_ref[...].astype(BF16)
    rc = a_ref.shape[0] // EPI_CHUNKS
    for r in range(EPI_CHUNKS):
        sl = slice(r * rc, (r + 1) * rc)
        acc = _dot_nt(a_ref[sl, :], w)
        o_ref[sl, :] = _rope_apply(acc, cos_ref[sl, :], sin_ref[sl, :]).astype(o_ref.dtype)


def _proj_kernel(a_ref, w_ref, o_ref):
    w = w_ref[0] if len(w_ref.shape) == 3 else w_ref[...]
    o_ref[...] = _dot_nt(a_ref[...], w).astype(o_ref.dtype)


def _proj_a(h, w_in, layer, cos_t, sin_t, seq):
    n, d = h.shape
    tm, tn = MM_TM, MM_TN
    per_b = seq // tm

    def variant(j):
        return jnp.where(j < 4, 0, jnp.where((j == 6) | (j == 8), 1, 2))

    return pl.pallas_call(
        _proj_rope_kernel,
        grid=(n // tm, SEG_A // tn),
        in_specs=[
            pl.BlockSpec((tm, d), lambda i, j: (i, 0)),
            pl.BlockSpec((None, tn, d), lambda i, j: (layer, j, 0)),
            pl.BlockSpec((None, tm, HEAD_DIM), lambda i, j: (variant(j), i % per_b, 0)),
            pl.BlockSpec((None, tm, HEAD_DIM), lambda i, j: (variant(j), i % per_b, 0)),
        ],
        out_specs=pl.BlockSpec((tm, tn), lambda i, j: (i, j)),
        out_shape=jax.ShapeDtypeStruct((n, SEG_A), BF16),
        compiler_params=_cparams(("arbitrary", "arbitrary")),
        name="proj_qkv",
    )(h, w_in, cos_t, sin_t)


def _proj_gate(h, w_in, layer):
    n, d = h.shape
    tm = MM_TM
    return pl.pallas_call(
        _proj_kernel,
        grid=(n // tm,),
        in_specs=[
            pl.BlockSpec((tm, d), lambda i: (i, 0)),
            pl.BlockSpec((None, LANES, d), lambda i: (layer, SEG_A // LANES, 0)),
        ],
        out_specs=pl.BlockSpec((tm, LANES), lambda i: (i, 0)),
        out_shape=jax.ShapeDtypeStruct((n, LANES), F32),
        compiler_params=_cparams(("arbitrary",)),
        name="proj_gate",
    )(h, w_in)


def _proj_r(h, w_in, layer):
    n, d = h.shape
    tm, tn = MM_TM, MM_TN
    return pl.pallas_call(
        _proj_kernel,
        grid=(n // tm, SEG_R // tn),
        in_specs=[
            pl.BlockSpec((tm, d), lambda i, j: (i, 0)),
            pl.BlockSpec((pl.Element(1), pl.Element(tn), pl.Element(d)), lambda i, j: (layer, pl.multiple_of(SEG_R0 + j * tn, 8), 0)),
        ],
        out_specs=pl.BlockSpec((tm, tn), lambda i, j: (i, j)),
        out_shape=jax.ShapeDtypeStruct((n, SEG_R), F32),
        compiler_params=_cparams(("arbitrary", "arbitrary")),
        name="proj_rnn_mix",
    )(h, w_in)


def _cmp_kernel(z_ref, w1_ref, w2_ref, pos_ref, cos_ref, sin_ref, o_ref):
    z = z_ref[...]
    w1 = w1_ref[...].astype(BF16)
    half = w1.shape[0] // 2
    a = _bdot(z, w1[:half])
    b = _bdot(z, w1[half:])
    pos = jnp.broadcast_to(pos_ref[...], (8, pos_ref.shape[1])).astype(BF16)
    bias = _bdot(pos, w1)[0:1]
    y = a + pltpu.roll(b, b.shape[0] - 1, 0) + bias
    hid = _gelu_tanh(y).astype(BF16)
    out = _bdot(hid, w2_ref[...].astype(BF16))
    groups = out.shape[0] // cos_ref.shape[0]
    cos = jnp.concatenate([cos_ref[...]] * groups, axis=0)
    sin = jnp.concatenate([sin_ref[...]] * groups, axis=0)
    o_ref[...] = _rope_apply(out, cos, sin).astype(o_ref.dtype)


def _compress(z2, w_cmp1, w_cmp2, cmp_pos, layer, cos_c, sin_c):
    _, b, rows, width = z2.shape
    n_grp = cos_c.shape[1]
    return pl.pallas_call(
        _cmp_kernel,
        grid=(2, b),
        in_specs=[
            pl.BlockSpec((None, None, rows, width), lambda j, bi: (j, bi, 0, 0)),
            pl.BlockSpec((None, None, 2 * width, CMP_HIDDEN), lambda j, bi: (layer, j, 0, 0)),
            pl.BlockSpec((None, None, CMP_HIDDEN, HEAD_DIM), lambda j, bi: (layer, j, 0, 0)),
            pl.BlockSpec((None, None, 1, 2 * width), lambda j, bi: (layer, j, 0, 0)),
            pl.BlockSpec((None, n_grp, HEAD_DIM), lambda j, bi: (j, 0, 0)),
            pl.BlockSpec((None, n_grp, HEAD_DIM), lambda j, bi: (j, 0, 0)),
        ],
        out_specs=pl.BlockSpec((None, None, rows, HEAD_DIM), lambda j, bi: (j, bi, 0, 0)),
        out_shape=jax.ShapeDtypeStruct((2, b, rows, HEAD_DIM), BF16),
        compiler_params=_cparams(("arbitrary", "arbitrary")),
        name="nsa_compress",
    )(z2, w_cmp1, w_cmp2, cmp_pos.reshape(cmp_pos.shape[0], 2, 1, CMP_BLOCK * HEAD_DIM), cos_c, sin_c)


def _attn_t_kernel(q_ref, kc_ref, vc_ref, ks_ref, vst_ref, kw_ref, vwt_ref, gl_ref, ovt_ref, ext_ref, o_ref):
    tq, tk = ATT_TQ, ATT_TK
    hpg = HEADS_PER_GROUP
    rows = hpg * tq
    i = pl.program_id(2)
    qb = q_ref[...]
    q4 = jnp.concatenate([qb[:, h * HEAD_DIM:(h + 1) * HEAD_DIM] for h in range(hpg)], axis=0)
    dn_nt = (((1,), (1,)), ((), ()))

    def tile_heads(x):
        return jnp.concatenate([x] * hpg, axis=1)

    n_cmp = kc_ref.shape[0] - 1
    key_n = lax.broadcasted_iota(jnp.int32, (LANES, tq), 0)
    t_pos = i * tq + lax.broadcasted_iota(jnp.int32, (LANES, tq), 1)
    cmask = (key_n < n_cmp) & (key_n * CMP_STRIDE + (CMP_BLOCK - 1) <= t_pos)
    cb4 = tile_heads(jnp.where(cmask, 0.0, NEG_INF))
    s = lax.dot_general(kc_ref[...], q4, dn_nt, preferred_element_type=F32) + cb4
    m = jnp.max(s, axis=0, keepdims=True)
    e = jnp.where(cb4 > -1.0, jnp.exp(s - m), 0.0)
    den = jnp.sum(e, axis=0, keepdims=True)
    p = e / jnp.maximum(den, 1e-30)
    vc_t = jnp.transpose(vc_ref[...].astype(F32)).astype(BF16)
    o_cmp = _bdot(vc_t, p.astype(BF16))
    p_sum = p[:, 0:tq]
    for h in range(1, hpg):
        p_sum = p_sum + p[:, h * tq:(h + 1) * tq]
    p_hi, p_lo = _split_bf16(p_sum)
    n_slc = ext_ref.shape[0] * (tk // SEL_BLOCK)
    imp = (_bdot(ovt_ref[...], p_hi) + _bdot(ovt_ref[...], p_lo))[:n_slc]

    blk = lax.broadcasted_iota(jnp.int32, (n_slc, tq), 0)
    cur = lax.shift_right_logical(i * tq + lax.broadcasted_iota(jnp.int32, (n_slc, tq), 1), int(math.log2(SEL_BLOCK)))
    visible = blk <= cur
    forced = (blk == 0) | (visible & (blk > cur - N_LOCAL_SEL))
    score = jnp.where(visible, jnp.where(forced, FORCED_SCORE, imp), NEG_INF)
    rank = jnp.zeros((n_slc, tq), F32)
    for k in range(n_slc):
        row = score[k:k + 1, :]
        beats = (row > score) | ((row == score) & (blk > k))
        rank = rank + jnp.where(beats, 1.0, 0.0)
    sel = jnp.where((rank < float(min(N_SELECT, n_slc))) & visible, 1.0, 0.0)
    sel = jnp.concatenate([sel, jnp.zeros((LANES - n_slc, tq), F32)], axis=0).astype(BF16)

    rel = (lax.broadcasted_iota(jnp.int32, (tk, tq), 1) + i * tq) - lax.broadcasted_iota(jnp.int32, (tk, tq), 0)

    def flash_step(c, carry, k_ref, vt_ref, bias):
        m_p, l_p, acc_p = carry
        off = pl.multiple_of(c * tk, tk)
        kk = k_ref[pl.ds(off, tk), :]
        sc = lax.dot_general(kk, q4, dn_nt, preferred_element_type=F32) + tile_heads(bias)
        m_n = jnp.maximum(m_p, jnp.max(sc, axis=0, keepdims=True))
        alpha = jnp.exp(m_p - m_n)
        pp = jnp.exp(sc - m_n)
        l_n = alpha * l_p + jnp.sum(pp, axis=0, keepdims=True)
        acc_n = alpha * acc_p + _bdot(vt_ref[c], pp.astype(BF16))
        return m_n, l_n, acc_n

    init = (jnp.full((1, rows), NEG_INF, F32), jnp.zeros((1, rows), F32), jnp.zeros((HEAD_DIM, rows), F32))

    def sel_body(c, carry):
        kmask = _bdot(ext_ref[c], sel) > 0.5
        ok = kmask & (rel - c * tk >= 0)
        return flash_step(c, carry, ks_ref, vst_ref, jnp.where(ok, 0.0, NEG_INF))

    _, l_s, acc_s = lax.fori_loop(0, i + 1, sel_body, init)

    def win_body(c, carry):
        d = rel - c * tk
        ok = (d >= 0) & (d < WINDOW)
        return flash_step(c, carry, kw_ref, vwt_ref, jnp.where(ok, 0.0, NEG_INF))

    c0 = jnp.maximum(i - WINDOW // tk, 0)
    _, l_w, acc_w = lax.fori_loop(c0, i + 1, win_body, init)

    o_sel = acc_s / l_s
    o_win = acc_w / l_w
    gates = _sigmoid(gl_ref[...])
    outs = []
    for h in range(hpg):
        sl = slice(h * tq, (h + 1) * tq)
        o_h = (gates[3 * h:3 * h + 1] * o_cmp[:, sl] + gates[3 * h + 1:3 * h + 2] * o_sel[:, sl]
               + gates[3 * h + 2:3 * h + 3] * o_win[:, sl])
        outs.append(jnp.transpose(o_h))
    o_ref[...] = jnp.concatenate(outs, axis=1).astype(o_ref.dtype)


def _attention_t(za, kvc, vst, vwt, gl_t, ov_t, ex_t, batch, seq):
    n = za.shape[0]
    tq = ATT_TQ
    per_b = seq // tq
    g = N_KV_HEADS
    n_cb = kvc.shape[2] // g
    kv_blk = lambda col0: pl.BlockSpec((seq, HEAD_DIM), lambda b, gi, i: (b, col0 + gi))
    vt_blk = pl.BlockSpec((None, None) + vst.shape[2:], lambda b, gi, i: (b, gi, 0, 0, 0))
    q_cols = Q_WIDTH // HEAD_DIM
    return pl.pallas_call(
        _attn_t_kernel,
        grid=(batch, g, per_b),
        in_specs=[
            pl.BlockSpec((tq, HEADS_PER_GROUP * HEAD_DIM), lambda b, gi, i: (b * per_b + i, gi)),
            pl.BlockSpec((None, None, n_cb, HEAD_DIM), lambda b, gi, i: (0, b, gi, 0)),
            pl.BlockSpec((None, None, n_cb, HEAD_DIM), lambda b, gi, i: (1, b, gi, 0)),
            kv_blk(q_cols + 2 * g), vt_blk, kv_blk(q_cols + 4 * g), vt_blk,
            pl.BlockSpec((None, gl_t.shape[1], tq), lambda b, gi, i: (gi, 0, b * per_b + i)),
            pl.BlockSpec((LANES, LANES), lambda b, gi, i: (0, 0)),
            pl.BlockSpec(ex_t.shape, lambda b, gi, i: (0, 0, 0)),
        ],
        out_specs=pl.BlockSpec((tq, HEADS_PER_GROUP * HEAD_DIM), lambda b, gi, i: (b * per_b + i, gi)),
        out_shape=jax.ShapeDtypeStruct((n, Q_WIDTH), BF16),
        compiler_params=_cparams(("arbitrary", "arbitrary", "arbitrary")),
        name="nsa_attention",
    )(za, kvc, kvc, za, vst, za, vwt, gl_t, ov_t, ex_t)


def _rglru_kernel(x_ref, g_ref, p_ref, wg_ref, o_ref, xbuf, hc):
    tt, cb = x_ref.shape
    step = pl.program_id(2)

    @pl.when(step == 0)
    def _():
        xbuf[0:8, :] = jnp.zeros((8, cb), F32)
        hc[...] = jnp.zeros_like(hc)

    x = x_ref[...]
    xbuf[8:8 + tt, :] = x
    prm = p_ref[...]
    xc = prm[4:5] + prm[3:4] * x
    for k in range(CONV_WIDTH - 1):
        shift = CONV_WIDTH - 1 - k
        xc = xc + prm[k:k + 1] * xbuf[8 - shift:8 - shift + tt, :]
    xbuf[0:8, :] = x[tt - 8:tt]

    xcb = xc.astype(BF16)
    pre_r, pre_i = [], []
    for h in range(cb // HEAD_DIM):
        xh = xcb[:, h * HEAD_DIM:(h + 1) * HEAD_DIM]
        pre_r.append(_bdot(xh, wg_ref[0, h].astype(BF16)))
        pre_i.append(_bdot(xh, wg_ref[1, h].astype(BF16)))
    r = _sigmoid(jnp.concatenate(pre_r, axis=1) + prm[5:6])
    ig = _sigmoid(jnp.concatenate(pre_i, axis=1) + prm[6:7])
    zl = -prm[7:8]
    softplus = jnp.maximum(zl, 0.0) + jnp.log1p(jnp.exp(-jnp.abs(zl)))
    log_a = (-LRU_C) * r * softplus
    a = jnp.exp(log_a)
    b = jnp.sqrt(1.0 - a * a) * (ig * xc)

    row = lax.broadcasted_iota(jnp.int32, (tt, cb), 0)
    s = 1
    while s < tt:
        keep = row >= s
        a_sh = jnp.where(keep, pltpu.roll(a, s, 0), 1.0)
        b_sh = jnp.where(keep, pltpu.roll(b, s, 0), 0.0)
        b = a * b_sh + b
        a = a * a_sh
        s *= 2
    hcur = b + a * hc[0:1, :]
    hc[...] = jnp.broadcast_to(hcur[tt - 1:tt], hc.shape)
    o_ref[...] = (hcur * _gelu_tanh(g_ref[...])).astype(o_ref.dtype)


def _rglru(zr, prm, w_gate, batch, seq):
    n = zr.shape[0]
    tt, cb = RNN_TT, RNN_CB
    per_b = seq // tt
    n_cb = RNN_WIDTH // cb
    hpb = cb // HEAD_DIM
    return pl.pallas_call(
        _rglru_kernel,
        grid=(batch, n_cb, per_b),
        in_specs=[
            pl.BlockSpec((tt, cb), lambda b, c, t: (b * per_b + t, c)),
            pl.BlockSpec((tt, cb), lambda b, c, t: (b * per_b + t, n_cb + c)),
            pl.BlockSpec((8, cb), lambda b, c, t: (0, c)),
            pl.BlockSpec((2, hpb, HEAD_DIM, HEAD_DIM), lambda b, c, t: (0, c, 0, 0)),
        ],
        out_specs=pl.BlockSpec((tt, cb), lambda b, c, t: (b * per_b + t, c)),
        out_shape=jax.ShapeDtypeStruct((n, RNN_WIDTH), BF16),
        scratch_shapes=[pltpu.VMEM((tt + 8, cb), F32), pltpu.VMEM((8, cb), F32)],
        compiler_params=_cparams(("arbitrary", "arbitrary", "arbitrary")),
        name="rglru",
    )(zr, zr, prm, w_gate)


def _mix_kernel(a1_ref, a2_ref, w1_ref, w2_ref, m1_ref, m2_ref, o_ref):
    w1 = w1_ref[...].astype(BF16)
    w2 = w2_ref[...].astype(BF16)
    rc = a1_ref.shape[0] // EPI_CHUNKS
    for r in range(EPI_CHUNKS):
        sl = slice(r * rc, (r + 1) * rc)
        y1 = _bdot(a1_ref[sl, :], w1)
        y2 = _bdot(a2_ref[sl, :], w2)
        o_ref[sl, :] = (_sigmoid(m1_ref[sl, :]) * y1 + _sigmoid(m2_ref[sl, :]) * y2).astype(o_ref.dtype)


def _mix(o_attn, o_rnn, w_branch, layer, zr):
    n, k = o_attn.shape
    tm, tn = MM_TM, MM_TN
    m_off = 2 * RNN_WIDTH // tn
    return pl.pallas_call(
        _mix_kernel,
        grid=(n // tm, D_MODEL // tn),
        in_specs=[
            pl.BlockSpec((tm, k), lambda i, j: (i, 0)),
            pl.BlockSpec((tm, k), lambda i, j: (i, 0)),
            pl.BlockSpec((None, None, k, tn), lambda i, j: (layer, 0, 0, j)),
            pl.BlockSpec((None, None, k, tn), lambda i, j: (layer, 1, 0, j)),
            pl.BlockSpec((tm, tn), lambda i, j: (i, m_off + j)),
            pl.BlockSpec((tm, tn), lambda i, j: (i, m_off + D_MODEL // tn + j)),
        ],
        out_specs=pl.BlockSpec((tm, tn), lambda i, j: (i, j)),
        out_shape=jax.ShapeDtypeStruct((n, D_MODEL), BF16),
        compiler_params=_cparams(("arbitrary", "arbitrary")),
        name="mixer_merge",
    )(o_attn, o_rnn, w_branch, w_branch, zr, zr)


def _out_kernel(a_ref, w_ref, x_ref, g_ref, o_ref):
    o_ref[...] = x_ref[...] + g_ref[...] * _bdot(a_ref[...], w_ref[...].astype(BF16))


def _out_proj(mixed, w_out, layer, x, gate, seq):
    n, d = x.shape
    tm, tn = MM_TM, MM_TN
    per_b = seq // tm
    return pl.pallas_call(
        _out_kernel,
        grid=(n // tm, d // tn),
        in_specs=[
            pl.BlockSpec((tm, d), lambda i, j: (i, 0)),
            pl.BlockSpec((None, d, tn), lambda i, j: (layer, 0, j)),
            pl.BlockSpec((tm, tn), lambda i, j: (i, j)),
            pl.BlockSpec((None, 1, tn), lambda i, j: (i // per_b, 0, j)),
        ],
        out_specs=pl.BlockSpec((tm, tn), lambda i, j: (i, j)),
        out_shape=jax.ShapeDtypeStruct((n, d), F32),
        compiler_params=_cparams(("arbitrary", "arbitrary")),
        name="out_proj",
    )(mixed, w_out, x, gate)


def _route_kernel(x_ref, g_ref, sc_ref, sh_ref, wr_ref, br_ref, h_ref, meta_ref, cnt_ref, run):
    tm = x_ref.shape[0]

    @pl.when(pl.program_id(0) == 0)
    def _():
        run[...] = jnp.zeros_like(run)

    h = _norm_mod(x_ref[...], g_ref[...], sc_ref[...], sh_ref[...])
    h_ref[...] = h
    h_hi, h_lo = _split_bf16(h)
    w_hi, w_lo = _split_bf16(wr_ref[...])
    logits = _bdot(h_hi, w_hi) + _bdot(h_hi, w_lo) + _bdot(h_lo, w_hi) + br_ref[...]
    lane = lax.broadcasted_iota(jnp.int32, (tm, LANES), 1)
    lane_f = lane.astype(F32)
    cur = jnp.where(lane < N_EXPERTS, logits, -jnp.inf)
    vals, idxs = [], []
    for _ in range(TOP_K):
        mx = jnp.max(cur, axis=-1, keepdims=True)
        ix = jnp.min(jnp.where(cur == mx, lane_f, float(LANES)), axis=-1, keepdims=True)
        vals.append(mx)
        idxs.append(ix)
        cur = jnp.where(lane_f == ix, -jnp.inf, cur)
    exps = [jnp.exp(v - vals[0]) for v in vals]
    den = exps[0]
    for e in exps[1:]:
        den = den + e
    hot = [lane_f == ix for ix in idxs]
    cnt = jnp.zeros((tm, LANES), F32)
    for hk in hot:
        cnt = cnt + jnp.where(hk, 1.0, 0.0)
    tri = jnp.where(lax.broadcasted_iota(jnp.int32, (tm, tm), 0) > lax.broadcasted_iota(jnp.int32, (tm, tm), 1),
                    1.0, 0.0).astype(BF16)
    pos = _bdot(tri, cnt.astype(BF16)) + run[0:1, :]
    meta = jnp.zeros((tm, LANES), F32)
    for k in range(TOP_K):
        pk = jnp.sum(jnp.where(hot[k], pos, 0.0), axis=-1, keepdims=True)
        meta = jnp.where(lane == k, idxs[k], meta)
        meta = jnp.where(lane == TOP_K + k, pk, meta)
        meta = jnp.where(lane == 2 * TOP_K + k, exps[k] / den, meta)
    meta_ref[...] = meta
    run[...] = run[...] + jnp.sum(cnt, axis=0, keepdims=True)
    cnt_ref[...] = run[...]


def _route(x, g, sc, sh, w_router, b_router, seq):
    n, d = x.shape
    tm = NORM_TM
    per_b = seq // tm
    wr = jnp.zeros((d, LANES), F32).at[:, :N_EXPERTS].set(w_router)
    br = jnp.zeros((1, LANES), F32).at[0, :N_EXPERTS].set(b_router)
    return pl.pallas_call(
        _route_kernel,
        grid=(n // tm,),
        in_specs=[
            pl.BlockSpec((tm, d), lambda i: (i, 0)),
            pl.BlockSpec((1, d), lambda i: (0, 0)),
            pl.BlockSpec((None, 1, d), lambda i: (i // per_b, 0, 0)),
            pl.BlockSpec((None, 1, d), lambda i: (i // per_b, 0, 0)),
            pl.BlockSpec((d, LANES), lambda i: (0, 0)),
            pl.BlockSpec((1, LANES), lambda i: (0, 0)),
        ],
        out_specs=[
            pl.BlockSpec((tm, d), lambda i: (i, 0)),
            pl.BlockSpec((tm, LANES), lambda i: (i, 0)),
            pl.BlockSpec((8, LANES), lambda i: (0, 0)),
        ],
        out_shape=[
            jax.ShapeDtypeStruct((n, d), F32),
            jax.ShapeDtypeStruct((n, LANES), F32),
            jax.ShapeDtypeStruct((8, LANES), F32),
        ],
        scratch_shapes=[pltpu.VMEM((8, LANES), F32)],
        compiler_params=_cparams(("arbitrary",)),
        name="moe_route",
    )(x, g.reshape(1, d), sc, sh, wr, br)


def _expert_kernel(te_ref, nt_ref, tok_ref, h_hbm, w1_hbm, b1_ref, w2_hbm, b2_ref, even_ref, o_ref,
                   xbuf, sem, w1s, w2s, stg1, stg2, wsem, *, layer):
    tm = o_ref.shape[0]
    j = pl.program_id(0)
    nt = nt_ref[0]
    nbuf = xbuf.shape[0]
    ahead = nbuf - 1
    cur = j % nbuf

    def load_weights(src, dst, stg):
        rows = stg.shape[1]
        n_chunks = dst.shape[0] // rows

        def chunk(ch):
            return pltpu.make_async_copy(src.at[pl.ds(ch * rows, rows), :], stg.at[ch % 2], wsem.at[ch % 2])

        chunk(0).start()
        for ch in range(n_chunks):
            if ch + 1 < n_chunks:
                chunk(ch + 1).start()
            chunk(ch).wait()
            dst[ch * rows:(ch + 1) * rows, :] = stg[ch % 2].astype(BF16)

    def start_gather(tile, buf):
        base = tile * tm

        def body(ro, c):
            r0 = pl.multiple_of(ro * 8, 8)
            for ri in range(8):
                tok = tok_ref[base + r0 + ri]
                pltpu.make_async_copy(h_hbm.at[pl.ds(tok, 1), :], xbuf.at[buf, pl.ds(r0 + ri, 1), :],
                                      sem.at[buf]).start()
            return c
        lax.fori_loop(0, tm // 8, body, 0)

    def wait_gather(buf):
        pltpu.make_async_copy(h_hbm.at[pl.ds(0, tm), :], xbuf.at[buf], sem.at[buf]).wait()

    @pl.when(j == 0)
    def _():
        for t in range(ahead):
            start_gather(jnp.minimum(t, nt - 1), t)

    expert = te_ref[j]
    new_expert = (j == 0) | (te_ref[jnp.maximum(j - 1, 0)] != expert)

    @pl.when((j < nt) & new_expert)
    def _():
        load_weights(w1_hbm.at[layer, expert], w1s, stg1)
        load_weights(w2_hbm.at[layer, expert], w2s, stg2)

    @pl.when(j < nt)
    def _():
        wait_gather(cur)
        x = xbuf[cur].astype(BF16)
        z = _bdot(x, w1s[...]) + b1_ref[...]
        z_next = pltpu.roll(z, z.shape[1] - 1, 1)
        glu = jnp.minimum(z, SWIGLU_LIMIT)
        lin = jnp.clip(z_next, -SWIGLU_LIMIT, SWIGLU_LIMIT)
        act = (glu * _sigmoid(SWIGLU_ALPHA * glu) * (lin + 1.0)).astype(BF16)
        act = _bdot(act, even_ref[...]).astype(BF16)
        o_ref[...] = _bdot(act, w2s[...]) + b2_ref[...]
        nxt = jnp.minimum(j + ahead, nt - 1) * tm
        nbf = (j + ahead) % nbuf
        for r in range(tm):
            tok = tok_ref[nxt + r]
            pltpu.make_async_copy(h_hbm.at[pl.ds(tok, 1), :], xbuf.at[nbf, pl.ds(r, 1), :], sem.at[nbf]).start()

    @pl.when((j >= nt) & (j < nt + ahead))
    def _():
        wait_gather(cur)

    @pl.when(j >= nt)
    def _():
        o_ref[...] = jnp.zeros_like(o_ref)


def _experts(h2, tile_expert, n_tiles, tok, w1, b1, w2, b2, layer):
    n, d = h2.shape
    tm = MOE_TM
    max_tiles = tile_expert.shape[0]
    ff2 = w1.shape[3]
    ff = ff2 // 2
    even = np.zeros((ff2, ff), np.float32)
    even[2 * np.arange(ff), np.arange(ff)] = 1.0
    e_map = lambda j, te, nt, tk: (layer, te[j], 0, 0)
    stg_rows1 = MOE_STAGE_BYTES // (4 * ff2)
    stg_rows2 = MOE_STAGE_BYTES // (4 * d)
    return pl.pallas_call(
        functools.partial(_expert_kernel, layer=layer),
        grid_spec=pltpu.PrefetchScalarGridSpec(
            num_scalar_prefetch=3,
            grid=(max_tiles,),
            in_specs=[
                pl.BlockSpec(memory_space=pl.ANY),
                pl.BlockSpec(memory_space=pl.ANY),
                pl.BlockSpec((None, None, 1, ff2), e_map),
                pl.BlockSpec(memory_space=pl.ANY),
                pl.BlockSpec((None, None, 1, d), e_map),
                pl.BlockSpec((ff2, ff), lambda j, te, nt, tk: (0, 0)),
            ],
            out_specs=pl.BlockSpec((tm, d), lambda j, te, nt, tk: (j, 0)),
            scratch_shapes=[
                pltpu.VMEM((MOE_GATHER_BUFS, tm, d), F32), pltpu.SemaphoreType.DMA((MOE_GATHER_BUFS,)),
                pltpu.VMEM((d, ff2), BF16), pltpu.VMEM((ff, d), BF16),
                pltpu.VMEM((2, stg_rows1, ff2), F32), pltpu.VMEM((2, stg_rows2, d), F32),
                pltpu.SemaphoreType.DMA((2,)),
            ],
        ),
        out_shape=jax.ShapeDtypeStruct((max_tiles * tm, d), F32),
        compiler_params=_cparams(("arbitrary",)),
        name="moe_experts",
    )(tile_expert, n_tiles, tok, h2, w1, b1, w2, b2, jnp.asarray(even, BF16))


def _combine_kernel(slot_ref, y_hbm, x_ref, w_ref, g_ref, gf_ref, o_ref, ybuf, sem, *, final_norm):
    tt = x_ref.shape[0]
    i = pl.program_id(0)
    nsteps = pl.num_programs(0)
    cur = i % 2

    def start_gather(step, buf):
        base = step * (tt * TOP_K)

        def body(ro, c):
            r0 = pl.multiple_of(ro * 8, 8)
            for ri in range(8):
                for k in range(TOP_K):
                    s = slot_ref[base + (r0 + ri) * TOP_K + k]
                    pltpu.make_async_copy(y_hbm.at[pl.ds(s, 1), :], ybuf.at[buf, k, pl.ds(r0 + ri, 1), :],
                                          sem.at[buf]).start()
            return c
        lax.fori_loop(0, tt // 8, body, 0)

    @pl.when(i == 0)
    def _():
        start_gather(0, 0)

    @pl.when(i + 1 < nsteps)
    def _():
        start_gather(i + 1, 1 - cur)

    for k in range(TOP_K):
        pltpu.make_async_copy(y_hbm.at[pl.ds(0, tt), :], ybuf.at[cur, k], sem.at[cur]).wait()
    w = w_ref[...]
    y = w[:, 0:1] * ybuf[cur, 0]
    for k in range(1, TOP_K):
        y = y + w[:, k:k + 1] * ybuf[cur, k]
    out = x_ref[...] + g_ref[...] * y
    if final_norm:
        r = lax.rsqrt(jnp.mean(out * out, axis=-1, keepdims=True) + NORM_EPS)
        out = (out * r) * gf_ref[...]
    o_ref[...] = out


def _combine(slots, y_slots, x, wts, gate, g_final, seq, final_norm):
    n, d = x.shape
    tt = COMB_TT
    per_b = seq // tt
    return pl.pallas_call(
        functools.partial(_combine_kernel, final_norm=final_norm),
        grid_spec=pltpu.PrefetchScalarGridSpec(
            num_scalar_prefetch=1,
            grid=(n // tt,),
            in_specs=[
                pl.BlockSpec(memory_space=pl.ANY),
                pl.BlockSpec((tt, d), lambda i, s: (i, 0)),
                pl.BlockSpec((tt, TOP_K), lambda i, s: (i, 0)),
                pl.BlockSpec((None, 1, d), lambda i, s: (i // per_b, 0, 0)),
                pl.BlockSpec((1, d), lambda i, s: (0, 0)),
            ],
            out_specs=pl.BlockSpec((tt, d), lambda i, s: (i, 0)),
            scratch_shapes=[pltpu.VMEM((2, TOP_K, tt, d), F32), pltpu.SemaphoreType.DMA((2,))],
        ),
        out_shape=jax.ShapeDtypeStruct((n, d), F32),
        compiler_params=_cparams(("arbitrary",)),
        name="moe_combine",
    )(slots, y_slots, x, wts, gate, g_final.reshape(1, d))


def _moe(x, g2, sc, sh, gate, w_router, b_router, w1, b1, w2, b2, layer, g_final, seq, final_norm):
    n, d = x.shape
    tm = MOE_TM
    max_tiles = (n * TOP_K + N_EXPERTS * (tm - 1)) // tm + MOE_GATHER_BUFS - 1
    h2, meta, cnt = _route(x, g2, sc, sh, w_router, b_router, seq)

    idx = meta[:, 0:TOP_K].astype(jnp.int32)
    pos = meta[:, TOP_K:2 * TOP_K].astype(jnp.int32)
    wts = meta[:, 2 * TOP_K:3 * TOP_K]
    counts = cnt[0, :N_EXPERTS].astype(jnp.int32)
    tiles_e = (counts + tm - 1) // tm
    tile_end = jnp.cumsum(tiles_e)
    offsets = (tile_end - tiles_e) * tm
    n_tiles = tile_end[-1:]
    tile_ids = jnp.arange(max_tiles, dtype=jnp.int32)
    tile_expert = jnp.minimum(jnp.sum((tile_end[None, :] <= tile_ids[:, None]).astype(jnp.int32), axis=1),
                              N_EXPERTS - 1)
    slot = offsets[idx] + pos
    tok_ids = jnp.broadcast_to(jnp.arange(n, dtype=jnp.int32)[:, None], slot.shape)
    tok = jnp.zeros((max_tiles * tm,), jnp.int32).at[slot.reshape(-1)].set(tok_ids.reshape(-1))

    y_slots = _experts(h2, tile_expert, n_tiles, tok, w1, b1, w2, b2, layer)
    return _combine(slot.reshape(-1), y_slots, x, wts, gate, g_final, seq, final_norm)


def _overlap_matrix(seq):
    n_cmp = (seq - CMP_BLOCK) // CMP_STRIDE + 1
    n_slc = seq // SEL_BLOCK
    cmp_start = np.arange(n_cmp) * CMP_STRIDE
    slc_start = np.arange(n_slc) * SEL_BLOCK
    ov = np.clip(np.minimum(cmp_start[:, None] + CMP_BLOCK, slc_start[None, :] + SEL_BLOCK)
                 - np.maximum(cmp_start[:, None], slc_start[None, :]), 0, None) / CMP_BLOCK
    out = np.zeros((LANES, LANES), np.float32)
    out[:n_slc, :n_cmp] = ov.T
    return jnp.asarray(out, BF16)


def _expand_matrix(seq):
    n_chunks = seq // ATT_TK
    key = np.arange(seq).reshape(n_chunks, ATT_TK, 1)
    blk = np.arange(LANES).reshape(1, 1, LANES)
    return jnp.asarray((key // SEL_BLOCK == blk).astype(np.float32), BF16)


def _cmp_rope_tables(seq):
    n_cmp = (seq - CMP_BLOCK) // CMP_STRIDE + 1
    half = ROPE_DIM // 2
    inv_freq = jnp.power(ROPE_THETA, -jnp.arange(half, dtype=F32) * 2.0 / ROPE_DIM)
    pos = jnp.arange(n_cmp + 1, dtype=F32) * CMP_STRIDE + (CMP_BLOCK - 1)
    ang = pos[:, None] * inv_freq[None, :]
    cos, sin = jnp.cos(ang), jnp.sin(ang)
    ones = jnp.ones((n_cmp + 1, HEAD_DIM - ROPE_DIM), F32)
    c_k = jnp.concatenate([cos, cos, ones], axis=1)
    s_k = jnp.concatenate([-sin, sin, 0.0 * ones], axis=1)
    return jnp.stack([c_k, jnp.ones_like(c_k)]), jnp.stack([s_k, jnp.zeros_like(s_k)])


def kernel(x, c, w_ada, b_ada, ada_table, g_norm1, w_in, w_cmp1, w_cmp2, cmp_pos, conv_w, conv_b,
           w_lru_gate, b_lru_gate, lru_lambda, w_branch, w_out, g_norm2, w_router, b_router,
           w_exp1, b_exp1, w_exp2, b_exp2, g_final):
    batch, seq, d = x.shape
    depth = w_in.shape[0]
    n = batch * seq
    g = N_KV_HEADS
    assert d == D_MODEL and seq % MM_TM == 0 and seq % ATT_TQ == 0 and n % COMB_TT == 0

    mod = _ada_mod(c, w_ada, b_ada, ada_table)
    cos_t, sin_t = _rope_tables(seq)
    cos_c, sin_c = _cmp_rope_tables(seq)
    ov_t = _overlap_matrix(seq)
    ex_t = _expand_matrix(seq)
    w_in = jnp.swapaxes(w_in, 1, 2)
    b1_e = b_exp1[:, :, None, :]
    b2_e = b_exp2[:, :, None, :]

    xf = x.reshape(n, d)
    for l in range(depth):
        m = [mod[l, :batch, j * d:(j + 1) * d].reshape(batch, 1, d) for j in range(N_ADA)]
        shift1, scale1, gate1, shift2, scale2, gate2 = m

        h = _norm_call(xf, g_norm1[l], scale1, shift1, seq)
        za = _proj_a(h, w_in, l, cos_t, sin_t, seq)
        zg = _proj_gate(h, w_in, l)
        zr = _proj_r(h, w_in, l)

        kv = za[:, Q_WIDTH:Q_WIDTH + 2 * KV_WIDTH].reshape(batch, seq // CMP_STRIDE, CMP_STRIDE, 2, g, HEAD_DIM)
        z2 = kv.transpose(3, 0, 4, 1, 2, 5).reshape(2, batch, g * (seq // CMP_STRIDE), CMP_STRIDE * HEAD_DIM)
        kvc = _compress(z2, w_cmp1, w_cmp2, cmp_pos, l, cos_c, sin_c)
        gl_t = zg[:, :GATE_W].reshape(n, g, 3 * HEADS_PER_GROUP).transpose(1, 2, 0)
        gl_t = jnp.pad(gl_t, ((0, 0), (0, 16 - 3 * HEADS_PER_GROUP), (0, 0)))

        def values_t(col0):
            v = za[:, col0:col0 + KV_WIDTH].reshape(batch, seq // ATT_TK, ATT_TK, g, HEAD_DIM)
            return v.transpose(0, 3, 1, 4, 2)

        vst = values_t(Q_WIDTH + 3 * KV_WIDTH)
        vwt = values_t(Q_WIDTH + 5 * KV_WIDTH)
        o_attn = _attention_t(za, kvc, vst, vwt, gl_t, ov_t, ex_t, batch, seq)

        prm = jnp.concatenate([conv_w[l], conv_b[l][None], b_lru_gate[l], lru_lambda[l][None]], axis=0)
        o_rnn = _rglru(zr, prm, w_lru_gate[l], batch, seq)

        mixed = _mix(o_attn, o_rnn, w_branch, l, zr)
        xf = _out_proj(mixed, w_out, l, xf, gate1, seq)

        xf = _moe(xf, g_norm2[l], scale2, shift2, gate2, w_router[l], b_router[l], w_exp1, b1_e, w_exp2, b2_e, l,
                  g_final, seq, final_norm=(l == depth - 1))
    return xf.reshape(batch, seq, d)
```
